```python
import jax, jax.numpy as jnp
from jax import lax
import numpy as np


D_MODEL = 1024
BATCH = 4
SEQ = 8192
DEPTH = 2

GRID_W = 64
CTX_LEN = 256
N_MIXERS = 2
NORM_EPS = 1e-6
ATTN_HEADS = 16
ATTN_KV_HEADS = 4
HEAD_DIM = D_MODEL // ATTN_HEADS
ATTN_GROUP = ATTN_HEADS // ATTN_KV_HEADS
Q_WIDTH = ATTN_HEADS * HEAD_DIM
KV_WIDTH = ATTN_KV_HEADS * HEAD_DIM
QKV_WIDTH = Q_WIDTH + 2 * KV_WIDTH
AXIS_DIM = HEAD_DIM // 2
ROPE_THETA = 10000.0
Q_BLOCK = 128
HGRN_WIDTH = D_MODEL
HGRN_EXPAND = 128
HGRN_HEADS = HGRN_WIDTH // HGRN_EXPAND
HGRN_DK = HGRN_EXPAND
HGRN_DV = HGRN_WIDTH // HGRN_HEADS
HGRN_CHUNK = 64
FFN_HIDDEN = ((8 * D_MODEL // 3 + 255) // 256) * 256
N_ATTN_LAYERS = (DEPTH + 1) // 2
N_HGRN_LAYERS = DEPTH // 2

kernel_name = 'hybrid_gqa_hgrn2_adaln_prefix_dit'


def rms_norm(x, w):
    xf = x.astype(jnp.float32)
    y = xf * lax.rsqrt(jnp.mean(xf * xf, axis=-1, keepdims=True) + NORM_EPS)
    return (y * w.astype(jnp.float32)).astype(x.dtype)


def modulate(h, shift, scale):
    return h * (1.0 + scale) + shift


def axial_rope_tables(n_tokens):
    rows = n_tokens // GRID_W
    row_pos = jnp.repeat(jnp.arange(rows), GRID_W).astype(jnp.float32)
    col_pos = jnp.tile(jnp.arange(GRID_W), rows).astype(jnp.float32)
    inv_freq = ROPE_THETA ** (-(jnp.arange(AXIS_DIM // 2, dtype=jnp.float32) * 2.0 / AXIS_DIM))
    ang_r = row_pos[:, None] * inv_freq
    ang_c = col_pos[:, None] * inv_freq
    return (jnp.cos(ang_r), jnp.sin(ang_r), jnp.cos(ang_c), jnp.sin(ang_c))


def rotate_half(x, cos, sin):
    x1, x2 = jnp.split(x, 2, axis=-1)
    return jnp.concatenate([x1 * cos - x2 * sin, x2 * cos + x1 * sin], axis=-1)


def apply_axial_rope(x, rope):
    cos_r, sin_r, cos_c, sin_c = rope
    xf = x.astype(jnp.float32)
    xr, xc = jnp.split(xf, 2, axis=-1)
    out = jnp.concatenate([rotate_half(xr, cos_r, sin_r), rotate_half(xc, cos_c, sin_c)], axis=-1)
    return out.astype(x.dtype)


def gqa_attend(q, k, v):
    s = jnp.einsum('bkgqd,bksd->bkgqs', q, k, preferred_element_type=jnp.float32) * (HEAD_DIM ** -0.5)
    p = jax.nn.softmax(s, axis=-1).astype(v.dtype)
    return jnp.einsum('bkgqs,bksd->bkgqd', p, v)


def attention_mixer(h_lat, h_ctx, w_qkv, q_norm, k_norm, w_o, rope, with_ctx_out):
    b, n, _ = h_lat.shape
    m = h_ctx.shape[1]
    w_q, w_kv = w_qkv[:, :Q_WIDTH], w_qkv[:, Q_WIDTH:]

    def queries(h):
        t = h.shape[1]
        q = (h @ w_q).reshape(b, t, ATTN_KV_HEADS, ATTN_GROUP, HEAD_DIM).transpose(0, 2, 3, 1, 4)
        return rms_norm(q, q_norm)

    def keys_values(h):
        t = h.shape[1]
        k, v = jnp.split(h @ w_kv, 2, axis=-1)
        k = k.reshape(b, t, ATTN_KV_HEADS, HEAD_DIM).transpose(0, 2, 1, 3)
        v = v.reshape(b, t, ATTN_KV_HEADS, HEAD_DIM).transpose(0, 2, 1, 3)
        return rms_norm(k, k_norm), v

    q_l = apply_axial_rope(queries(h_lat), rope)
    k_l, v_l = keys_values(h_lat)
    k_l = apply_axial_rope(k_l, rope)
    k_c, v_c = keys_values(h_ctx)
    k_all = jnp.concatenate([k_c, k_l], axis=2)
    v_all = jnp.concatenate([v_c, v_l], axis=2)

    nb = n // Q_BLOCK
    q_blocks = q_l.reshape(b, ATTN_KV_HEADS, ATTN_GROUP, nb, Q_BLOCK, HEAD_DIM).transpose(3, 0, 1, 2, 4, 5)
    o_blocks = lax.map(lambda qb: gqa_attend(qb, k_all, v_all), q_blocks)
    o_l = o_blocks.transpose(1, 0, 4, 2, 3, 5).reshape(b, n, Q_WIDTH)
    y_l = o_l @ w_o
    if not with_ctx_out:
        return y_l, None
    o_c = gqa_attend(queries(h_ctx), k_c, v_c)
    y_c = o_c.transpose(0, 3, 1, 2, 4).reshape(b, m, Q_WIDTH) @ w_o
    return y_l, y_c


def hgrn2_chunk_scan(q, k, log_f, v, s0):
    b, h, n, dk = q.shape
    dv = v.shape[-1]
    nc = n // HGRN_CHUNK

    def to_chunks(a):
        return a.reshape(b, h, nc, HGRN_CHUNK, a.shape[-1]).transpose(2, 0, 1, 3, 4)

    mask = jnp.tril(jnp.ones((HGRN_CHUNK, HGRN_CHUNK), dtype=bool))[None, None, :, :, None]

    def step(state, inp):
        qc, kc, gc, vc = inp
        L = jnp.cumsum(gc, axis=2)
        o_inter = jnp.einsum('bhtk,bhkv->bhtv', qc * jnp.exp(L), state)
        diff = L[:, :, :, None, :] - L[:, :, None, :, :]
        decay = jnp.where(mask, jnp.exp(jnp.where(mask, diff, 0.0)), 0.0)
        a = jnp.einsum('bhtk,bhtsk,bhsk->bhts', qc, decay, kc)
        o_intra = jnp.einsum('bhts,bhsv->bhtv', a, vc)
        L_end = L[:, :, -1:, :]
        new_state = jnp.exp(L_end[:, :, 0, :])[..., None] * state + jnp.einsum(
            'bhsk,bhsv->bhkv', kc * jnp.exp(L_end - L), vc)
        return new_state, o_inter + o_intra

    s_final, o = lax.scan(step, s0, (to_chunks(q), to_chunks(k), to_chunks(log_f), to_chunks(v)))
    return o.transpose(1, 2, 0, 3, 4).reshape(b, h, n, dv), s_final


def hgrn2_final_state(k, log_f, v):
    L = jnp.cumsum(log_f, axis=2)
    return jnp.einsum('bhsk,bhsv->bhkv', k * jnp.exp(L[:, :, -1:, :] - L), v)


def hgrn2_mixer(h_lat, h_ctx, w_in, lb, out_norm, w_o, with_ctx_out):
    b = h_lat.shape[0]

    def heads(a):
        return a.reshape(b, a.shape[1], HGRN_HEADS, -1).transpose(0, 2, 1, 3).astype(jnp.float32)

    def gates(z):
        f = lb + (1.0 - lb) * jax.nn.sigmoid(z.astype(jnp.float32))
        return heads(jnp.log(f)), heads(1.0 - f)

    def recurrent_inputs(h):
        z_fw, z_bw, v = jnp.split(h @ w_in[:, 2 * HGRN_WIDTH:], 3, axis=-1)
        g_fw, k_fw = gates(z_fw)
        g_bw, k_bw = gates(z_bw)
        return g_fw, k_fw, g_bw, k_bw, heads(v)

    def query_gate(h):
        q, g = jnp.split(h @ w_in[:, :2 * HGRN_WIDTH], 2, axis=-1)
        return heads(q), g

    def flip(a):
        return jnp.flip(a, axis=2)

    def readout(o, g):
        t = o.shape[2]
        o = rms_norm(o.transpose(0, 2, 1, 3), out_norm.reshape(HGRN_HEADS, HGRN_DV))
        o = o * jax.nn.sigmoid(g.astype(jnp.float32)).reshape(b, t, HGRN_HEADS, HGRN_DV)
        return o.reshape(b, t, HGRN_WIDTH).astype(h_lat.dtype) @ w_o

    zeros = jnp.zeros((b, HGRN_HEADS, HGRN_DK, HGRN_DV), jnp.float32)
    gc_fw, kc_fw, gc_bw, kc_bw, vc = recurrent_inputs(h_ctx)
    y_c = None
    if with_ctx_out:
        qc, gate_c = query_gate(h_ctx)
        oc_fw, s_fw = hgrn2_chunk_scan(qc, kc_fw, gc_fw, vc, zeros)
        oc_bw, s_bw = hgrn2_chunk_scan(flip(qc), flip(kc_bw), flip(gc_bw), flip(vc), zeros)
        y_c = readout(oc_fw + flip(oc_bw), gate_c)
    else:
        s_fw = hgrn2_final_state(kc_fw, gc_fw, vc)
        s_bw = hgrn2_final_state(flip(kc_bw), flip(gc_bw), flip(vc))

    q, gate = query_gate(h_lat)
    g_fw, k_fw, g_bw, k_bw, v = recurrent_inputs(h_lat)
    o_fw, _ = hgrn2_chunk_scan(q, k_fw, g_fw, v, s_fw)
    o_bw, _ = hgrn2_chunk_scan(flip(q), flip(k_bw), flip(g_bw), flip(v), s_bw)
    return readout(o_fw + flip(o_bw), gate), y_c


def swiglu(h, w_in, w_out):
    a, u = jnp.split(h @ w_in, 2, axis=-1)
    return (jax.nn.silu(a) * u) @ w_out


def setup_inputs(seed: int = 0) -> dict:
    key = jax.random.key(seed)
    ks = jax.random.split(key, 19)
    D = D_MODEL

    def nrm(k, shape, scale):
        return jax.random.normal(k, shape, jnp.float32) * scale

    return {
        'x': nrm(ks[0], (BATCH, SEQ, D), 1.0),
        'c': nrm(ks[1], (BATCH, D), 1.0),
        'ctx': nrm(ks[2], (BATCH, CTX_LEN, D), 1.0),
        'c_ctx': nrm(ks[3], (D,), 1.0),
        'ada_w': nrm(ks[4], (DEPTH, D, 6 * D), 0.5 * D ** -0.5),
        'ada_b': nrm(ks[5], (DEPTH, 6 * D), 0.02),
        'norm_mix_w': 1.0 + nrm(ks[6], (DEPTH, D), 0.05),
        'norm_ffn_w': 1.0 + nrm(ks[7], (DEPTH, D), 0.05),
        'attn_w_qkv': nrm(ks[8], (N_ATTN_LAYERS, D, QKV_WIDTH), D ** -0.5),
        'attn_q_norm': 1.0 + nrm(ks[9], (N_ATTN_LAYERS, HEAD_DIM), 0.05),
        'attn_k_norm': 1.0 + nrm(ks[10], (N_ATTN_LAYERS, HEAD_DIM), 0.05),
        'attn_w_o': nrm(ks[11], (N_ATTN_LAYERS, Q_WIDTH, D), Q_WIDTH ** -0.5),
        'hgrn_w_in': nrm(ks[12], (N_HGRN_LAYERS, D, 5 * HGRN_WIDTH), D ** -0.5),
        'hgrn_lb_logits': nrm(ks[13], (DEPTH, HGRN_WIDTH), 0.5),
        'hgrn_out_norm': 1.0 + nrm(ks[14], (N_HGRN_LAYERS, HGRN_WIDTH), 0.05),
        'hgrn_w_o': nrm(ks[15], (N_HGRN_LAYERS, HGRN_WIDTH, D), HGRN_WIDTH ** -0.5),
        'ffn_w_in': nrm(ks[16], (DEPTH, D, 2 * FFN_HIDDEN), D ** -0.5),
        'ffn_w_out': nrm(ks[17], (DEPTH, FFN_HIDDEN, D), FFN_HIDDEN ** -0.5),
        'final_norm_w': 1.0 + nrm(ks[18], (D,), 0.05),
    }


def reference(x, c, ctx, c_ctx, ada_w, ada_b, norm_mix_w, norm_ffn_w, attn_w_qkv, attn_q_norm,
              attn_k_norm, attn_w_o, hgrn_w_in, hgrn_lb_logits, hgrn_out_norm, hgrn_w_o,
              ffn_w_in, ffn_w_out, final_norm_w):
    n = x.shape[1]
    rope = axial_rope_tables(n)
    p = jax.nn.softmax(hgrn_lb_logits.astype(jnp.float32), axis=0)
    lower_bounds = jnp.cumsum(p, axis=0) - p[0:1]

    h, hc = x, ctx
    for i in range(DEPTH):
        last = i == DEPTH - 1
        mod_l = (jax.nn.silu(c) @ ada_w[i] + ada_b[i])[:, None, :]
        mod_c = (jax.nn.silu(c_ctx) @ ada_w[i] + ada_b[i])[None, None, :]
        sh1_l, sc1_l, gt1_l, sh2_l, sc2_l, gt2_l = jnp.split(mod_l, 6, axis=-1)
        sh1_c, sc1_c, gt1_c, sh2_c, sc2_c, gt2_c = jnp.split(mod_c, 6, axis=-1)

        hn_l = modulate(rms_norm(h, norm_mix_w[i]), sh1_l, sc1_l)
        hn_c = modulate(rms_norm(hc, norm_mix_w[i]), sh1_c, sc1_c)
        if i % N_MIXERS == 0:
            j = i // N_MIXERS
            y_l, y_c = attention_mixer(hn_l, hn_c, attn_w_qkv[j], attn_q_norm[j], attn_k_norm[j],
                                       attn_w_o[j], rope, not last)
        else:
            j = i // N_MIXERS
            y_l, y_c = hgrn2_mixer(hn_l, hn_c, hgrn_w_in[j], lower_bounds[i], hgrn_out_norm[j],
                                   hgrn_w_o[j], not last)

        h = h + gt1_l * y_l
        h = h + gt2_l * swiglu(modulate(rms_norm(h, norm_ffn_w[i]), sh2_l, sc2_l), ffn_w_in[i], ffn_w_out[i])
        if not last:
            hc = hc + gt1_c * y_c
            hc = hc + gt2_c * swiglu(modulate(rms_norm(hc, norm_ffn_w[i]), sh2_c, sc2_c),
                                     ffn_w_in[i], ffn_w_out[i])
    return rms_norm(h, final_norm_w)
```

```python
import functools

import numpy as np
import jax
import jax.numpy as jnp
from jax import lax
from jax.experimental import pallas as pl
from jax.experimental.pallas import tpu as pltpu

F32 = jnp.float32
BF16 = jnp.bfloat16

NORM_EPS = 1e-6
GRID_W = 64
ATTN_HEADS = 16
ATTN_KV_HEADS = 4
HEAD_DIM = 64
ATTN_GROUP = ATTN_HEADS // ATTN_KV_HEADS
ROPE_THETA = 10000.0
HGRN_HEADS = 8
HGRN_DK = 128

LANES = 128
VMEM_LIMIT = 56 * 1024 * 1024

HGRN_CHUNK = 128
HGRN_STEP = 512


def _params(*sem):
    return pltpu.CompilerParams(dimension_semantics=sem, vmem_limit_bytes=VMEM_LIMIT)


def _dot(a, b):
    return jnp.dot(a, b, preferred_element_type=F32)


def _dot_nt(a, b):
    return lax.dot_general(a, b, (((1,), (1,)), ((), ())), preferred_element_type=F32)


def _sigmoid(x):
    return 1.0 / (1.0 + jnp.exp(-x))


def _rms(x):
    return x * lax.rsqrt(jnp.mean(x * x, axis=-1, keepdims=True) + NORM_EPS)


def _adaln_kernel(c_ref, w_ref, b_ref, o_ref):
    c = c_ref[...]
    a = (c * _sigmoid(c)).astype(BF16)
    o_ref[...] = _dot(a, w_ref[...].astype(BF16)) + b_ref[...]


def _adaln(cc, ada_w, ada_b):
    depth, d, n = ada_w.shape
    rows = cc.shape[0]
    tn = 1536
    return pl.pallas_call(
        _adaln_kernel,
        grid=(depth, n // tn),
        in_specs=[
            pl.BlockSpec((rows, d), lambda i, j: (0, 0)),
            pl.BlockSpec((None, d, tn), lambda i, j: (i, 0, j)),
            pl.BlockSpec((None, 1, tn), lambda i, j: (i, 0, j)),
        ],
        out_specs=pl.BlockSpec((None, rows, tn), lambda i, j: (i, 0, j)),
        out_shape=jax.ShapeDtypeStruct((depth, rows, n), F32),
        compiler_params=_params("parallel", "parallel"),
        name="adaln",
    )(cc, ada_w, ada_b.reshape(depth, 1, n))


def _swap16(x):
    lane = lax.broadcasted_iota(jnp.int32, x.shape, 1)
    up = pltpu.roll(x, LANES - 16, axis=1)
    down = pltpu.roll(x, 16, axis=1)
    return jnp.where((lane % 32) < 16, up, down)


def _head_norm_rope(x, w, cos, sin):
    lane = lax.broadcasted_iota(jnp.int32, x.shape, 1)
    lo = lane < HEAD_DIM
    sq = x * x
    s_lo = jnp.sum(jnp.where(lo, sq, 0.0), axis=-1, keepdims=True)
    s_hi = jnp.sum(jnp.where(lo, 0.0, sq), axis=-1, keepdims=True)
    ms = jnp.where(lo, s_lo, s_hi) * (1.0 / HEAD_DIM)
    xn = x * lax.rsqrt(ms + NORM_EPS) * w
    if cos is None:
        return xn
    return xn * cos + _swap16(xn) * sin


def _qkv_kernel(*refs, rope, q_scale):
    if rope:
        (x_ref, nw_ref, sh_ref, sc_ref, w_ref, qn_ref, kn_ref, cos_ref, sin_ref,
         q_ref, kt_ref, v_ref) = refs
        cos, sin = cos_ref[...], sin_ref[...]
    else:
        x_ref, nw_ref, sh_ref, sc_ref, w_ref, qn_ref, kn_ref, q_ref, kt_ref, v_ref = refs
        cos = sin = None
    x = x_ref[...]
    hn = (_rms(x) * nw_ref[...]) * (1.0 + sc_ref[...]) + sh_ref[...]
    z = _dot(hn.astype(BF16), w_ref[...])
    qw = q_ref.shape[-1]
    kw = kt_ref.shape[0]
    for s in range(qw // LANES):
        xs = z[:, s * LANES:(s + 1) * LANES]
        q_ref[:, s * LANES:(s + 1) * LANES] = (
            _head_norm_rope(xs, qn_ref[...], cos, sin) * q_scale).astype(BF16)
    for s in range(kw // LANES):
        xs = z[:, qw + s * LANES:qw + (s + 1) * LANES]
        kn = _head_norm_rope(xs, kn_ref[...], cos, sin)
        kt_ref[s * LANES:(s + 1) * LANES, :] = kn.T.astype(BF16)
    for h in range(kw // HEAD_DIM):
        v_ref[h] = z[:, qw + kw + h * HEAD_DIM:qw + kw + (h + 1) * HEAD_DIM].astype(BF16)


def _qkv(h, nw, sh, sc, w_qkv, qn, kn, rope_tabs, tm):
    b, t, d = h.shape
    n = w_qkv.shape[1]
    qw = ATTN_HEADS * HEAD_DIM
    kw = ATTN_KV_HEADS * HEAD_DIM
    rope = rope_tabs is not None
    row = lambda i, j: (i, 0, 0)
    in_specs = [
        pl.BlockSpec((None, tm, d), lambda i, j: (i, j, 0)),
        pl.BlockSpec((1, d), lambda i, j: (0, 0)),
        pl.BlockSpec((None, 1, d), row),
        pl.BlockSpec((None, 1, d), row),
        pl.BlockSpec((d, n), lambda i, j: (0, 0)),
        pl.BlockSpec((1, LANES), lambda i, j: (0, 0)),
        pl.BlockSpec((1, LANES), lambda i, j: (0, 0)),
    ]
    args = [h, nw, sh, sc, w_qkv, qn, kn]
    if rope:
        in_specs += [pl.BlockSpec((tm, LANES), lambda i, j: (j, 0))] * 2
        args += list(rope_tabs)
    return pl.pallas_call(
        functools.partial(_qkv_kernel, rope=rope, q_scale=HEAD_DIM ** -0.5),
        grid=(b, t // tm),
        in_specs=in_specs,
        out_specs=[
            pl.BlockSpec((None, tm, qw), lambda i, j: (i, j, 0)),
            pl.BlockSpec((None, kw, tm), lambda i, j: (i, 0, j)),
            pl.BlockSpec((None, ATTN_KV_HEADS, tm, HEAD_DIM), lambda i, j: (i, 0, j, 0)),
        ],
        out_shape=[
            jax.ShapeDtypeStruct((b, t, qw), BF16),
            jax.ShapeDtypeStruct((b, kw, t), BF16),
            jax.ShapeDtypeStruct((b, ATTN_KV_HEADS, t, HEAD_DIM), BF16),
        ],
        compiler_params=_params("parallel", "parallel"),
        name="qkv_rope" if rope else "qkv_ctx",
    )(*args)


def _rope_tables(n_tokens):
    axis_dim = HEAD_DIM // 2
    pos = np.arange(n_tokens)
    inv_freq = ROPE_THETA ** (-(np.arange(axis_dim // 2, dtype=np.float32) * 2.0 / axis_dim))
    ang_r = (pos // GRID_W).astype(np.float32)[:, None] * inv_freq
    ang_c = (pos % GRID_W).astype(np.float32)[:, None] * inv_freq
    cos = np.concatenate([np.cos(ang_r)] * 2 + [np.cos(ang_c)] * 2, axis=1)
    sin = np.concatenate([-np.sin(ang_r), np.sin(ang_r), -np.sin(ang_c), np.sin(ang_c)], axis=1)
    reps = LANES // HEAD_DIM
    return (jnp.asarray(np.tile(cos, (1, reps)), F32), jnp.asarray(np.tile(sin, (1, reps)), F32))


def _attn_kernel(q_ref, kt_ref, v_ref, o_ref, *, tk):
    tq = q_ref.shape[0]
    nk = kt_ref.shape[1] // tk
    q = q_ref[...]
    qs = jnp.concatenate([q[:, g * HEAD_DIM:(g + 1) * HEAD_DIM] for g in range(ATTN_GROUP)], axis=0)
    rows = qs.shape[0]

    def body(j, carry):
        m, l, acc = carry
        start = pl.multiple_of(j * tk, tk)
        kt = kt_ref[:, pl.ds(start, tk)]
        vv = v_ref[pl.ds(start, tk), :]
        s = _dot(qs, kt)
        m_new = jnp.maximum(m, jnp.max(s, axis=-1, keepdims=True))
        p = jnp.exp(s - m_new)
        alpha = jnp.exp(m - m_new)
        l = alpha * l + jnp.sum(p, axis=-1, keepdims=True)
        acc = alpha * acc + _dot(p.astype(BF16), vv)
        return m_new, l, acc

    init = (jnp.full((rows, 1), -jnp.inf, F32), jnp.zeros((rows, 1), F32),
            jnp.zeros((rows, HEAD_DIM), F32))
    m, l, acc = lax.fori_loop(0, nk, body, init)
    o = acc / l
    o_ref[...] = jnp.concatenate(
        [o[g * tq:(g + 1) * tq] for g in range(ATTN_GROUP)], axis=1).astype(BF16)


def _attention(q, kt, v, tq, tk):
    b, t, qw = q.shape
    s = kt.shape[2]
    gw = ATTN_GROUP * HEAD_DIM
    return pl.pallas_call(
        functools.partial(_attn_kernel, tk=tk),
        grid=(b, ATTN_KV_HEADS, t // tq),
        in_specs=[
            pl.BlockSpec((None, tq, gw), lambda i, k, j: (i, j, k)),
            pl.BlockSpec((None, HEAD_DIM, s), lambda i, k, j: (i, k, 0)),
            pl.BlockSpec((None, None, s, HEAD_DIM), lambda i, k, j: (i, k, 0, 0)),
        ],
        out_specs=pl.BlockSpec((None, tq, gw), lambda i, k, j: (i, j, k)),
        out_shape=jax.ShapeDtypeStruct((b, t, qw), BF16),
        compiler_params=_params("parallel", "parallel", "parallel"),
        name="attention",
    )(q, kt, v)


def _post_kernel(*refs, mode, final, ffn_splits):
    if mode == "attn":
        (h_ref, y_ref, wo_ref, g1_ref, nw_ref, sh_ref, sc_ref, g2_ref, wi_ref, wout_ref) = refs[:10]
        rest = refs[10:]
        y_in = y_ref[...]
    else:
        (h_ref, of_ref, ob_ref, gate_ref, on_ref, wo_ref, g1_ref, nw_ref, sh_ref, sc_ref, g2_ref,
         wi_ref, wout_ref) = refs[:13]
        rest = refs[13:]
        o = of_ref[...] + ob_ref[...]
        gate = gate_ref[...]
        parts = []
        for hh in range(o.shape[1] // HGRN_DK):
            sl = slice(hh * HGRN_DK, (hh + 1) * HGRN_DK)
            parts.append(_rms(o[:, sl]) * on_ref[:, sl] * _sigmoid(gate[:, sl]))
        y_in = jnp.concatenate(parts, axis=1).astype(BF16)
    if final:
        fw_ref, out_ref = rest
    else:
        (out_ref,) = rest
    h1 = h_ref[...] + g1_ref[...] * _dot(y_in, wo_ref[...])
    hn = ((_rms(h1) * nw_ref[...]) * (1.0 + sc_ref[...]) + sh_ref[...]).astype(BF16)
    hidden = wout_ref.shape[0]
    acc = None
    off = 0
    for width in ffn_splits:
        a = _dot(hn, wi_ref[:, off:off + width])
        u = _dot(hn, wi_ref[:, hidden + off:hidden + off + width])
        gu = (a * _sigmoid(a) * u).astype(BF16)
        part = _dot(gu, wout_ref[off:off + width, :])
        acc = part if acc is None else acc + part
        off += width
    h2 = h1 + g2_ref[...] * acc
    if final:
        h2 = _rms(h2) * fw_ref[...]
    out_ref[...] = h2


def _post(h, mixer_in, wo, g1, nw, sh, sc, g2, wi, wout, final_w, mode, tm):
    b, t, d = h.shape
    hidden = wout.shape[0]
    tile = pl.BlockSpec((None, tm, d), lambda i, j: (i, j, 0))
    row = pl.BlockSpec((None, 1, d), lambda i, j: (i, 0, 0))
    const = lambda shape: pl.BlockSpec(shape, lambda i, j: (0,) * len(shape),
                                       pipeline_mode=pl.Buffered(1))
    if mode == "attn":
        mix_specs = [pl.BlockSpec((None, tm, mixer_in[0].shape[-1]), lambda i, j: (i, j, 0))]
    else:
        mix_specs = [tile, tile, tile, const((1, d))]
    in_specs = [tile] + mix_specs + [const(wo.shape), row, const((1, d)), row, row, row,
                                     const(wi.shape), const(wout.shape)]
    args = [h] + list(mixer_in) + [wo, g1, nw, sh, sc, g2, wi, wout]
    final = final_w is not None
    if final:
        in_specs.append(const((1, d)))
        args.append(final_w)
    splits = (1536, hidden - 1536) if hidden > 1536 else (hidden,)
    return pl.pallas_call(
        functools.partial(_post_kernel, mode=mode, final=final, ffn_splits=splits),
        grid=(b, t // tm),
        in_specs=in_specs,
        out_specs=tile,
        out_shape=jax.ShapeDtypeStruct((b, t, d), F32),
        compiler_params=_params("parallel", "parallel"),
        name="post_" + mode + ("_final" if final else ""),
    )(*args)


def _hgrn_in_kernel(x_ref, nw_ref, sh_ref, sc_ref, w_ref, lbl_ref,
                    q_ref, gate_ref, gf_ref, kf_ref, gb_ref, kb_ref, v_ref, *, layer):
    x = x_ref[...]
    hn = ((_rms(x) * nw_ref[...]) * (1.0 + sc_ref[...]) + sh_ref[...]).astype(BF16)
    wd = q_ref.shape[-1]
    lg = lbl_ref[...]
    e = jnp.exp(lg - jnp.max(lg, axis=0, keepdims=True))
    p = e / jnp.sum(e, axis=0, keepdims=True)
    lb = jnp.sum(p[1:layer + 1], axis=0, keepdims=True) if layer > 0 else jnp.zeros_like(p[0:1])

    def proj(i):
        return _dot(hn, w_ref[:, i * wd:(i + 1) * wd])

    q_ref[...] = proj(0).astype(BF16)
    gate_ref[...] = proj(1)
    for i, (g_ref, k_ref) in ((2, (gf_ref, kf_ref)), (3, (gb_ref, kb_ref))):
        f = lb + (1.0 - lb) * _sigmoid(proj(i))
        g_ref[...] = jnp.log(f)
        k_ref[...] = (1.0 - f).astype(BF16)
    v_ref[...] = proj(4).astype(BF16)


def _hgrn_in(h, nw, sh, sc, w_in, lb_logits, layer, tm):
    b, t, d = h.shape
    wd = w_in.shape[1] // 5
    tile = lambda: pl.BlockSpec((None, tm, wd), lambda i, j: (i, j, 0))
    row = pl.BlockSpec((None, 1, d), lambda i, j: (i, 0, 0))
    depth = lb_logits.shape[0]
    dts = (BF16, F32, F32, BF16, F32, BF16, BF16)
    return pl.pallas_call(
        functools.partial(_hgrn_in_kernel, layer=layer),
        grid=(b, t // tm),
        in_specs=[
            pl.BlockSpec((None, tm, d), lambda i, j: (i, j, 0)),
            pl.BlockSpec((1, d), lambda i, j: (0, 0)),
            row, row,
            pl.BlockSpec(w_in.shape, lambda i, j: (0, 0), pipeline_mode=pl.Buffered(1)),
            pl.BlockSpec((depth, wd), lambda i, j: (0, 0)),
        ],
        out_specs=[tile() for _ in dts],
        out_shape=[jax.ShapeDtypeStruct((b, t, wd), dt) for dt in dts],
        compiler_params=_params("parallel", "parallel"),
        name="hgrn_in",
    )(h, nw, sh, sc, w_in, lb_logits)


def _chunk_tables(c, rev):
    levels = int(np.log2(c))
    idx = np.arange(c)
    t, j = idx[:, None], idx[None, :]
    mats = [(j <= t), (j > t)]
    masks = [np.eye(c, dtype=bool)]
    for lv in range(1, levels + 1):
        blk = 2 ** lv
        mid = (idx // blk) * blk + blk // 2
        upper = (idx % blk) >= blk // 2
        m_up = upper[:, None] & (j >= mid[:, None]) & (j <= t)
        m_lo = (~upper)[:, None] & (j > t) & (j <= mid[:, None] - 1)
        mats.append(m_up | m_lo)
        same = (idx[:, None] // blk) == (idx[None, :] // blk)
        masks.append(same & upper[:, None] & (~upper)[None, :])
    mats = np.stack(mats).astype(np.float32)
    masks = np.stack(masks).astype(np.float32)
    if rev:
        mats = mats[:, ::-1, ::-1]
        masks = masks[:, ::-1, ::-1]
    return (jnp.asarray(mats.reshape(-1, c), BF16), jnp.asarray(masks, F32))


def _split_bf16(g):
    hi = g.astype(BF16)
    return hi, (g - hi.astype(F32)).astype(BF16)


def _hgrn_chunk(q, k, g, v, st, mst, masks, rev):
    c = q.shape[0]
    g_hi, g_lo = _split_bf16(g)
    e = _dot(mst, g_hi) + _dot(mst, g_lo)
    end = e[0:1] if rev else e[c - 1:c]
    dec = jnp.exp(e)
    qf, kf = q.astype(F32), k.astype(F32)
    o = _dot_nt((qf * dec[0:c]).astype(BF16), st.astype(BF16))
    a = masks[0] * _dot_nt(q, k)
    for lv in range(1, masks.shape[0]):
        d = dec[(lv + 1) * c:(lv + 2) * c]
        a = a + masks[lv] * _dot_nt((qf * d).astype(BF16), (kf * d).astype(BF16))
    o = o + _dot(a.astype(BF16), v)
    kr = (kf * dec[c:2 * c]).astype(BF16)
    st_new = st * jnp.exp(end) + _dot(v.astype(F32).T.astype(BF16), kr)
    return o, st_new


def _hgrn_state_kernel(kf_ref, gf_ref, kb_ref, gb_ref, v_ref, tri_ref, sf_ref, sb_ref):
    v = v_ref[...]
    vt = v.astype(F32).T.astype(BF16)
    n = v.shape[0]
    tri = tri_ref[...]
    for d, (k_ref, g_ref, s_ref) in enumerate(((kf_ref, gf_ref, sf_ref), (kb_ref, gb_ref, sb_ref))):
        g_hi, g_lo = _split_bf16(g_ref[...])
        cum = _dot(tri[d], g_hi) + _dot(tri[d], g_lo)
        end = cum[n - 1:n] if d == 0 else cum[0:1]
        kr = (k_ref[...].astype(F32) * jnp.exp(end - cum)).astype(BF16)
        s_ref[...] = _dot(vt, kr)


def _hgrn_state(kf, gf, kb, gb, v):
    b, n, w = v.shape
    heads = w // HGRN_DK
    idx = np.arange(n)
    tri = jnp.asarray(np.stack([idx[None, :] <= idx[:, None], idx[None, :] >= idx[:, None]]), BF16)
    tile = pl.BlockSpec((None, n, HGRN_DK), lambda i, hh: (i, 0, hh))
    st = pl.BlockSpec((None, None, HGRN_DK, HGRN_DK), lambda i, hh: (i, hh, 0, 0))
    shape = jax.ShapeDtypeStruct((b, heads, HGRN_DK, HGRN_DK), F32)
    return pl.pallas_call(
        _hgrn_state_kernel,
        grid=(b, heads),
        in_specs=[tile] * 5 + [pl.BlockSpec((2, n, n), lambda i, hh: (0, 0, 0))],
        out_specs=[st, st],
        out_shape=[shape, shape],
        compiler_params=_params("parallel", "parallel"),
        name="hgrn_state",
    )(kf, gf, kb, gb, v, tri)


def _hgrn_scan_kernel(qf_ref, vf_ref, kf_ref, gf_ref, qb_ref, vb_ref, kb_ref, gb_ref,
                      s0f_ref, s0b_ref, mf_ref, mb_ref, mkf_ref, mkb_ref,
                      of_ref, ob_ref, stf_ref, stb_ref, *, chunk):
    @pl.when(pl.program_id(2) == 0)
    def _():
        stf_ref[...] = s0f_ref[...]
        stb_ref[...] = s0b_ref[...]

    n_chunks = qf_ref.shape[0] // chunk
    st_f, st_b = stf_ref[...], stb_ref[...]
    mf, mb = mf_ref[...], mb_ref[...]
    for ci in range(n_chunks):
        sl = slice(ci * chunk, (ci + 1) * chunk)
        o, st_f = _hgrn_chunk(qf_ref[sl, :], kf_ref[sl, :], gf_ref[sl, :], vf_ref[sl, :],
                              st_f, mf, mkf_ref, False)
        of_ref[sl, :] = o
        rc = n_chunks - 1 - ci
        sl = slice(rc * chunk, (rc + 1) * chunk)
        o, st_b = _hgrn_chunk(qb_ref[sl, :], kb_ref[sl, :], gb_ref[sl, :], vb_ref[sl, :],
                              st_b, mb, mkb_ref, True)
        ob_ref[sl, :] = o
    stf_ref[...] = st_f
    stb_ref[...] = st_b


def _hgrn_scan(q, v, kf, gf, kb, gb, s0f, s0b, step, chunk):
    b, n, w = q.shape
    heads = w // HGRN_DK
    nt = n // step
    mf, mkf = _chunk_tables(chunk, False)
    mb, mkb = _chunk_tables(chunk, True)
    fwd = pl.BlockSpec((None, step, HGRN_DK), lambda i, hh, t: (i, t, hh))
    bwd = pl.BlockSpec((None, step, HGRN_DK), lambda i, hh, t: (i, nt - 1 - t, hh))
    st = pl.BlockSpec((None, None, HGRN_DK, HGRN_DK), lambda i, hh, t: (i, hh, 0, 0))
    c2 = pl.BlockSpec(mf.shape, lambda i, hh, t: (0, 0))
    c3 = pl.BlockSpec(mkf.shape, lambda i, hh, t: (0, 0, 0))
    out = jax.ShapeDtypeStruct((b, n, w), F32)
    return pl.pallas_call(
        functools.partial(_hgrn_scan_kernel, chunk=chunk),
        grid=(b, heads, nt),
        in_specs=[fwd] * 4 + [bwd] * 4 + [st, st, c2, c2, c3, c3],
        out_specs=[fwd, bwd],
        out_shape=[out, out],
        scratch_shapes=[pltpu.VMEM((HGRN_DK, HGRN_DK), F32)] * 2,
        compiler_params=_params("parallel", "parallel", "arbitrary"),
        name="hgrn_scan",
    )(q, v, kf, gf, q, v, kb, gb, s0f, s0b, mf, mb, mkf, mkb)


def kernel(x, c, ctx, c_ctx, ada_w, ada_b, norm_mix_w, norm_ffn_w, attn_w_qkv, attn_q_norm,
           attn_k_norm, attn_w_o, hgrn_w_in, hgrn_lb_logits, hgrn_out_norm, hgrn_w_o,
           ffn_w_in, ffn_w_out, final_norm_w):
    b, n, d = x.shape
    m = ctx.shape[1]
    depth = ada_w.shape[0]
    tm = min(512, n)

    rows = -(-(b + 1) // 8) * 8
    cc = jnp.zeros((rows, d), F32).at[:b].set(c).at[b].set(c_ctx)
    mod = _adaln(cc, ada_w, ada_b)

    def mods(i, lat):
        parts = []
        for s in range(6):
            blk = mod[i, :, s * d:(s + 1) * d]
            if lat:
                parts.append(blk[:b, None, :])
            else:
                parts.append(jnp.broadcast_to(blk[b][None, None, :], (b, 1, d)))
        return parts

    h, hc = x, ctx
    for i in range(depth):
        last = i == depth - 1
        j = i // 2
        sh1, sc1, gt1, sh2, sc2, gt2 = mods(i, True)
        csh1, csc1, cgt1, csh2, csc2, cgt2 = mods(i, False)
        nmw = norm_mix_w[i][None, :]
        nfw = norm_ffn_w[i][None, :]
        wi = ffn_w_in[i].astype(BF16)
        wout = ffn_w_out[i].astype(BF16)
        fin = final_norm_w[None, :] if last else None
        if i % 2 == 0:
            w_qkv = attn_w_qkv[j].astype(BF16)
            wo = attn_w_o[j].astype(BF16)
            qn = jnp.tile(attn_q_norm[j], LANES // HEAD_DIM)[None, :]
            kn = jnp.tile(attn_k_norm[j], LANES // HEAD_DIM)[None, :]
            q_l, kt_l, v_l = _qkv(h, nmw, sh1, sc1, w_qkv, qn, kn, _rope_tables(n), tm)
            q_c, kt_c, v_c = _qkv(hc, nmw, csh1, csc1, w_qkv, qn, kn, None, m)
            kt_all = jnp.concatenate([kt_c, kt_l], axis=2)
            v_all = jnp.concatenate([v_c, v_l], axis=2)
            o_l = _attention(q_l, kt_all, v_all, 256, 768 if (m + n) % 768 == 0 else m)
            h = _post(h, [o_l], wo, gt1, nfw, sh2, sc2, gt2, wi, wout, fin, "attn", tm)
            if not last:
                o_c = _attention(q_c, kt_c, v_c, m, m)
                hc = _post(hc, [o_c], wo, cgt1, nfw, csh2, csc2, cgt2, wi, wout, None, "attn", m)
        else:
            w_in = hgrn_w_in[j].astype(BF16)
            wo = hgrn_w_o[j].astype(BF16)
            on = hgrn_out_norm[j][None, :]
            q, gate, gf, kf, gb, kb, v = _hgrn_in(h, nmw, sh1, sc1, w_in, hgrn_lb_logits, i, tm)
            cq, cgate, cgf, ckf, cgb, ckb, cv = _hgrn_in(hc, nmw, csh1, csc1, w_in,
                                                         hgrn_lb_logits, i, m)
            if last:
                s0f, s0b = _hgrn_state(ckf, cgf, ckb, cgb, cv)
            else:
                zero = jnp.zeros((b, HGRN_HEADS, HGRN_DK, HGRN_DK), F32)
                oc_f, oc_b = _hgrn_scan(cq, cv, ckf, cgf, ckb, cgb, zero, zero, m, HGRN_CHUNK)
                s0f, s0b = _hgrn_state(ckf, cgf, ckb, cgb, cv)
                hc = _post(hc, [oc_f, oc_b, cgate, on], wo, cgt1, nfw, csh2, csc2, cgt2, wi, wout,
                           None, "hgrn", m)
            o_f, o_b = _hgrn_scan(q, v, kf, gf, kb, gb, s0f, s0b, min(HGRN_STEP, n), HGRN_CHUNK)
            h = _post(h, [o_f, o_b, gate, on], wo, gt1, nfw, sh2, sc2, gt2, wi, wout, fin, "hgrn", tm)
    return h
```

```python
import functools

import numpy as np
import jax
import jax.numpy as jnp
from jax import lax
from jax.experimental import pallas as pl
from jax.experimental.pallas import tpu as pltpu

F32 = jnp.float32
BF16 = jnp.bfloat16

NORM_EPS = 1e-6
GRID_W = 64
ATTN_HEADS = 16
ATTN_KV_HEADS = 4
HEAD_DIM = 64
ATTN_GROUP = ATTN_HEADS // ATTN_KV_HEADS
ROPE_THETA = 10000.0
HGRN_HEADS = 8
HGRN_DK = 128

LANES = 128
SUBLANES = 8
VMEM_LIMIT = 56 * 1024 * 1024

HGRN_CHUNK = 128
HGRN_STEP = 1024


def _params(*sem):
    return pltpu.CompilerParams(dimension_semantics=sem, vmem_limit_bytes=VMEM_LIMIT)


def _dot(a, b):
    return jnp.dot(a, b, preferred_element_type=F32)


def _dot_nt(a, b):
    return lax.dot_general(a, b, (((1,), (1,)), ((), ())), preferred_element_type=F32)


def _sigmoid(x):
    return 1.0 / (1.0 + jnp.exp(-x))


def _rms(x):
    return x * lax.rsqrt(jnp.mean(x * x, axis=-1, keepdims=True) + NORM_EPS)


def _adaln_kernel(c_ref, w_ref, b_ref, o_ref):
    c = c_ref[...]
    a = (c * _sigmoid(c)).astype(BF16)
    o_ref[...] = _dot(a, w_ref[...].astype(BF16)) + b_ref[...]


def _adaln(cc, ada_w, ada_b):
    depth, d, n = ada_w.shape
    rows = cc.shape[0]
    tn = 1536
    return pl.pallas_call(
        _adaln_kernel,
        grid=(depth, n // tn),
        in_specs=[
            pl.BlockSpec((rows, d), lambda i, j: (0, 0)),
            pl.BlockSpec((None, d, tn), lambda i, j: (i, 0, j)),
            pl.BlockSpec((None, 1, tn), lambda i, j: (i, 0, j)),
        ],
        out_specs=pl.BlockSpec((None, rows, tn), lambda i, j: (i, 0, j)),
        out_shape=jax.ShapeDtypeStruct((depth, rows, n), F32),
        compiler_params=_params("parallel", "parallel"),
        name="adaln",
    )(cc, ada_w, ada_b.reshape(depth, 1, n))


def _swap16(x):
    lane = lax.broadcasted_iota(jnp.int32, x.shape, 1)
    up = pltpu.roll(x, LANES - 16, axis=1)
    down = pltpu.roll(x, 16, axis=1)
    return jnp.where((lane % 32) < 16, up, down)


def _head_norm_rope(x, w, cos, sin):
    lane = lax.broadcasted_iota(jnp.int32, x.shape, 1)
    lo = lane < HEAD_DIM
    sq = x * x
    s_lo = jnp.sum(jnp.where(lo, sq, 0.0), axis=-1, keepdims=True)
    s_hi = jnp.sum(jnp.where(lo, 0.0, sq), axis=-1, keepdims=True)
    ms = jnp.where(lo, s_lo, s_hi) * (1.0 / HEAD_DIM)
    xn = x * lax.rsqrt(ms + NORM_EPS) * w
    if cos is None:
        return xn
    return xn * cos + _swap16(xn) * sin


def _qkv_kernel(*refs, rope, q_scale):
    if rope:
        (x_ref, nw_ref, sh_ref, sc_ref, w_ref, qn_ref, kn_ref, cos_ref, sin_ref,
         qt_ref, k_ref, vt_ref) = refs
        cos, sin = cos_ref[...], sin_ref[...]
    else:
        x_ref, nw_ref, sh_ref, sc_ref, w_ref, qn_ref, kn_ref, qt_ref, k_ref, vt_ref = refs
        cos = sin = None
    x = x_ref[...]
    hn = (_rms(x) * nw_ref[...]) * (1.0 + sc_ref[...]) + sh_ref[...]
    z = _dot(hn.astype(BF16), w_ref[...])
    tm = z.shape[0]
    qw = qt_ref.shape[0]
    kw = k_ref.shape[0] * HEAD_DIM
    for s in range(qw // LANES):
        xs = z[:, s * LANES:(s + 1) * LANES]
        qn = _head_norm_rope(xs, qn_ref[...], cos, sin) * q_scale
        qt_ref[s * LANES:(s + 1) * LANES, :] = qn.T.astype(BF16)
    for s in range(kw // LANES):
        xs = z[:, qw + s * LANES:qw + (s + 1) * LANES]
        kn = _head_norm_rope(xs, kn_ref[...], cos, sin)
        k_ref[2 * s] = kn[:, :HEAD_DIM].astype(BF16)
        k_ref[2 * s + 1] = kn[:, HEAD_DIM:].astype(BF16)
    row = lax.broadcasted_iota(jnp.int32, (LANES - HEAD_DIM, tm), 0)
    pad = jnp.where(row == 0, 1.0, 0.0).astype(BF16)
    for s in range(kw // LANES):
        vt = z[:, qw + kw + s * LANES:qw + kw + (s + 1) * LANES].T.astype(BF16)
        for half in range(2):
            vt_ref[2 * s + half, :HEAD_DIM, :] = vt[half * HEAD_DIM:(half + 1) * HEAD_DIM]
            vt_ref[2 * s + half, HEAD_DIM:, :] = pad


def _qkv(h, nw, sh, sc, w_qkv, qn, kn, rope_tabs, tm):
    b, t, d = h.shape
    n = w_qkv.shape[1]
    qw = ATTN_HEADS * HEAD_DIM
    kw = ATTN_KV_HEADS * HEAD_DIM
    rope = rope_tabs is not None
    row = lambda i, j: (i, 0, 0)
    in_specs = [
        pl.BlockSpec((None, tm, d), lambda i, j: (i, j, 0)),
        pl.BlockSpec((1, d), lambda i, j: (0, 0)),
        pl.BlockSpec((None, 1, d), row),
        pl.BlockSpec((None, 1, d), row),
        pl.BlockSpec((d, n), lambda i, j: (0, 0)),
        pl.BlockSpec((1, LANES), lambda i, j: (0, 0)),
        pl.BlockSpec((1, LANES), lambda i, j: (0, 0)),
    ]
    args = [h, nw, sh, sc, w_qkv, qn, kn]
    if rope:
        in_specs += [pl.BlockSpec((tm, LANES), lambda i, j: (j, 0))] * 2
        args += list(rope_tabs)
    return pl.pallas_call(
        functools.partial(_qkv_kernel, rope=rope, q_scale=HEAD_DIM ** -0.5 * np.log2(np.e)),
        grid=(b, t // tm),
        in_specs=in_specs,
        out_specs=[
            pl.BlockSpec((None, qw, tm), lambda i, j: (i, 0, j)),
            pl.BlockSpec((None, ATTN_KV_HEADS, tm, HEAD_DIM), lambda i, j: (i, 0, j, 0)),
            pl.BlockSpec((None, ATTN_KV_HEADS, LANES, tm), lambda i, j: (i, 0, 0, j)),
        ],
        out_shape=[
            jax.ShapeDtypeStruct((b, qw, t), BF16),
            jax.ShapeDtypeStruct((b, ATTN_KV_HEADS, t, HEAD_DIM), BF16),
            jax.ShapeDtypeStruct((b, ATTN_KV_HEADS, LANES, t), BF16),
        ],
        compiler_params=_params("parallel", "parallel"),
        name="qkv_rope" if rope else "qkv_ctx",
    )(*args)


def _rope_tables(n_tokens):
    axis_dim = HEAD_DIM // 2
    pos = np.arange(n_tokens)
    inv_freq = ROPE_THETA ** (-(np.arange(axis_dim // 2, dtype=np.float32) * 2.0 / axis_dim))
    ang_r = (pos // GRID_W).astype(np.float32)[:, None] * inv_freq
    ang_c = (pos % GRID_W).astype(np.float32)[:, None] * inv_freq
    cos = np.concatenate([np.cos(ang_r)] * 2 + [np.cos(ang_c)] * 2, axis=1)
    sin = np.concatenate([-np.sin(ang_r), np.sin(ang_r), -np.sin(ang_c), np.sin(ang_c)], axis=1)
    reps = LANES // HEAD_DIM
    return (jnp.asarray(np.tile(cos, (1, reps)), F32), jnp.asarray(np.tile(sin, (1, reps)), F32))


def _attn_kernel(qt_ref, k_ref, vt_ref, o_ref, sa_ref, sb_ref, *, tk):
    tq = qt_ref.shape[1]
    nk = k_ref.shape[0] // tk
    qt = qt_ref[...]
    qs = jnp.concatenate([qt[g * HEAD_DIM:(g + 1) * HEAD_DIM] for g in range(ATTN_GROUP)], axis=1)
    cols = qs.shape[1]

    def scores(j, dst_ref):
        start = pl.multiple_of(j * tk, tk)
        dst_ref[...] = _dot(k_ref[pl.ds(start, tk), :], qs)

    def accumulate(j, src_ref, carry):
        m, acc = carry
        start = pl.multiple_of(j * tk, tk)
        s = src_ref[...]
        m_new = jnp.maximum(m, jnp.max(s, axis=0, keepdims=True))
        p = jnp.exp2(s - m_new).astype(BF16)
        acc = jnp.exp2(m - m_new) * acc + _dot(vt_ref[:, pl.ds(start, tk)], p)
        return m_new, acc

    def pair(i, carry):
        scores(2 * i + 1, sb_ref)
        carry = accumulate(2 * i, sa_ref, carry)
        scores(2 * i + 2, sa_ref)
        return accumulate(2 * i + 1, sb_ref, carry)

    carry = (jnp.full((1, cols), -jnp.inf, F32), jnp.zeros((LANES, cols), F32))
    scores(0, sa_ref)
    carry = lax.fori_loop(0, (nk - 1) // 2, pair, carry)
    if nk % 2 == 1:
        carry = accumulate(nk - 1, sa_ref, carry)
    else:
        scores(nk - 1, sb_ref)
        carry = accumulate(nk - 2, sa_ref, carry)
        carry = accumulate(nk - 1, sb_ref, carry)
    acc = carry[1]
    o = (acc / acc[HEAD_DIM:HEAD_DIM + 1]).T
    o_ref[...] = jnp.concatenate(
        [o[g * tq:(g + 1) * tq, :HEAD_DIM] for g in range(ATTN_GROUP)], axis=1).astype(BF16)


def _attention(qt, k, vt, tq, tk):
    b, qw, t = qt.shape
    s = k.shape[2]
    gw = ATTN_GROUP * HEAD_DIM
    return pl.pallas_call(
        functools.partial(_attn_kernel, tk=tk),
        grid=(b, ATTN_KV_HEADS, t // tq),
        in_specs=[
            pl.BlockSpec((None, gw, tq), lambda i, k, j: (i, k, j)),
            pl.BlockSpec((None, None, s, HEAD_DIM), lambda i, k, j: (i, k, 0, 0)),
            pl.BlockSpec((None, None, LANES, s), lambda i, k, j: (i, k, 0, 0)),
        ],
        out_specs=pl.BlockSpec((None, tq, gw), lambda i, k, j: (i, j, k)),
        out_shape=jax.ShapeDtypeStruct((b, t, qw), BF16),
        scratch_shapes=[pltpu.VMEM((tk, ATTN_GROUP * tq), F32)] * 2,
        compiler_params=_params("parallel", "parallel", "parallel"),
        name="attention",
    )(qt, k, vt)


def _post_kernel(*refs, mode, final, ffn_splits):
    if mode == "attn":
        (h_ref, y_ref, wo_ref, g1_ref, nw_ref, sh_ref, sc_ref, g2_ref, wi_ref, wout_ref) = refs[:10]
        rest = refs[10:]
        y_in = y_ref[...]
    else:
        (h_ref, of_ref, ob_ref, gate_ref, on_ref, wo_ref, g1_ref, nw_ref, sh_ref, sc_ref, g2_ref,
         wi_ref, wout_ref) = refs[:13]
        rest = refs[13:]
        o = of_ref[...] + ob_ref[...]
        gate = gate_ref[...]
        parts = []
        for hh in range(o.shape[1] // HGRN_DK):
            sl = slice(hh * HGRN_DK, (hh + 1) * HGRN_DK)
            parts.append(_rms(o[:, sl]) * on_ref[:, sl] * _sigmoid(gate[:, sl]))
        y_in = jnp.concatenate(parts, axis=1).astype(BF16)
    if final:
        fw_ref, out_ref = rest
    else:
        (out_ref,) = rest
    h1 = h_ref[...] + g1_ref[...] * _dot(y_in, wo_ref[...])
    hn = ((_rms(h1) * nw_ref[...]) * (1.0 + sc_ref[...]) + sh_ref[...]).astype(BF16)
    hidden = wout_ref.shape[0]
    acc = None
    off = 0
    for width in ffn_splits:
        a = _dot(hn, wi_ref[:, off:off + width])
        u = _dot(hn, wi_ref[:, hidden + off:hidden + off + width])
        gu = (a * _sigmoid(a) * u).astype(BF16)
        part = _dot(gu, wout_ref[off:off + width, :])
        acc = part if acc is None else acc + part
        off += width
    h2 = h1 + g2_ref[...] * acc
    if final:
        h2 = _rms(h2) * fw_ref[...]
    out_ref[...] = h2


def _post(h, mixer_in, wo, g1, nw, sh, sc, g2, wi, wout, final_w, mode, tm):
    b, t, d = h.shape
    hidden = wout.shape[0]
    tile = pl.BlockSpec((None, tm, d), lambda i, j: (i, j, 0))
    row = pl.BlockSpec((None, 1, d), lambda i, j: (i, 0, 0))
    const = lambda shape: pl.BlockSpec(shape, lambda i, j: (0,) * len(shape),
                                       pipeline_mode=pl.Buffered(1))
    if mode == "attn":
        mix_specs = [pl.BlockSpec((None, tm, mixer_in[0].shape[-1]), lambda i, j: (i, j, 0))]
    else:
        mix_specs = [tile, tile, tile, const((1, d))]
    in_specs = [tile] + mix_specs + [const(wo.shape), row, const((1, d)), row, row, row,
                                     const(wi.shape), const(wout.shape)]
    args = [h] + list(mixer_in) + [wo, g1, nw, sh, sc, g2, wi, wout]
    final = final_w is not None
    if final:
        in_specs.append(const((1, d)))
        args.append(final_w)
    splits = (1536, hidden - 1536) if hidden > 1536 else (hidden,)
    return pl.pallas_call(
        functools.partial(_post_kernel, mode=mode, final=final, ffn_splits=splits),
        grid=(b, t // tm),
        in_specs=in_specs,
        out_specs=tile,
        out_shape=jax.ShapeDtypeStruct((b, t, d), F32),
        compiler_params=_params("parallel", "parallel"),
        name="post_" + mode + ("_final" if final else ""),
    )(*args)


def _hgrn_in_kernel(x_ref, nw_ref, sh_ref, sc_ref, w_ref, lbl_ref,
                    q_ref, gate_ref, gf_ref, kf_ref, gb_ref, kb_ref, v_ref, *, layer):
    x = x_ref[...]
    hn = ((_rms(x) * nw_ref[...]) * (1.0 + sc_ref[...]) + sh_ref[...]).astype(BF16)
    wd = q_ref.shape[-1]
    lg = lbl_ref[...]
    e = jnp.exp(lg - jnp.max(lg, axis=0, keepdims=True))
    p = e / jnp.sum(e, axis=0, keepdims=True)
    lb = jnp.sum(p[1:layer + 1], axis=0, keepdims=True) if layer > 0 else jnp.zeros_like(p[0:1])

    def proj(i):
        return _dot(hn, w_ref[:, i * wd:(i + 1) * wd])

    q_ref[...] = proj(0).astype(BF16)
    gate_ref[...] = proj(1)
    for i, (g_ref, k_ref) in ((2, (gf_ref, kf_ref)), (3, (gb_ref, kb_ref))):
        f = lb + (1.0 - lb) * _sigmoid(proj(i))
        g_ref[...] = jnp.log(f)
        k_ref[...] = (1.0 - f).astype(BF16)
    v_ref[...] = proj(4).astype(BF16)


def _hgrn_in(h, nw, sh, sc, w_in, lb_logits, layer, tm):
    b, t, d = h.shape
    wd = w_in.shape[1] // 5
    tile = lambda: pl.BlockSpec((None, tm, wd), lambda i, j: (i, j, 0))
    row = pl.BlockSpec((None, 1, d), lambda i, j: (i, 0, 0))
    depth = lb_logits.shape[0]
    dts = (BF16, F32, F32, BF16, F32, BF16, BF16)
    return pl.pallas_call(
        functools.partial(_hgrn_in_kernel, layer=layer),
        grid=(b, t // tm),
        in_specs=[
            pl.BlockSpec((None, tm, d), lambda i, j: (i, j, 0)),
            pl.BlockSpec((1, d), lambda i, j: (0, 0)),
            row, row,
            pl.BlockSpec(w_in.shape, lambda i, j: (0, 0), pipeline_mode=pl.Buffered(1)),
            pl.BlockSpec((depth, wd), lambda i, j: (0, 0)),
        ],
        out_specs=[tile() for _ in dts],
        out_shape=[jax.ShapeDtypeStruct((b, t, wd), dt) for dt in dts],
        compiler_params=_params("parallel", "parallel"),
        name="hgrn_in",
    )(h, nw, sh, sc, w_in, lb_logits)


def _chunk_tables(c, rev):
    levels = int(np.log2(c))
    idx = np.arange(c)
    t, j = idx[:, None], idx[None, :]
    mats = [(j <= t)]
    masks = [np.eye(c, dtype=bool)]
    for lv in range(1, levels + 1):
        blk = 2 ** lv
        mid = (idx // blk) * blk + blk // 2
        upper = (idx % blk) >= blk // 2
        m_up = upper[:, None] & (j >= mid[:, None]) & (j <= t)
        m_lo = (~upper)[:, None] & (j > t) & (j <= mid[:, None] - 1)
        if blk <= SUBLANES:
            mats.append(m_up | m_lo)
        same = (idx[:, None] // blk) == (idx[None, :] // blk)
        masks.append(same & upper[:, None] & (~upper)[None, :])
    mats = np.stack(mats).astype(np.float32)
    masks = np.stack(masks).astype(np.float32)
    if rev:
        mats = mats[:, ::-1, ::-1]
        masks = masks[:, ::-1, ::-1]
    return (jnp.asarray(mats.reshape(-1, c), BF16), jnp.asarray(masks, F32))


def _split_bf16(g):
    hi = g.astype(BF16)
    return hi, (g - hi.astype(F32)).astype(BF16)


def _level_exponents(cum, blk, rev):
    half = blk // 2
    parts = []
    for b0 in range(0, cum.shape[0], blk):
        lo, up = cum[b0:b0 + half], cum[b0 + half:b0 + blk]
        if rev:
            ref = cum[b0 + half:b0 + half + 1]
            parts += [lo - ref, ref - up]
        else:
            ref = cum[b0 + half - 1:b0 + half]
            parts += [ref - lo, up - ref]
    return jnp.concatenate(parts, axis=0)


def _hgrn_chunk(q, k, g, v, st, mst, masks, rev):
    c = q.shape[0]
    g_hi, g_lo = _split_bf16(g)
    e = _dot(mst, g_hi) + _dot(mst, g_lo)
    cum = e[0:c]
    end = cum[0:1] if rev else cum[c - 1:c]
    qf, kf = q.astype(F32), k.astype(F32)
    o = _dot_nt((qf * jnp.exp(cum)).astype(BF16), st.astype(BF16))
    a = masks[0] * _dot_nt(q, k)
    for lv in range(1, masks.shape[0]):
        blk = 2 ** lv
        ex = e[lv * c:(lv + 1) * c] if blk <= SUBLANES else _level_exponents(cum, blk, rev)
        d = jnp.exp(ex)
        a = a + masks[lv] * _dot_nt((qf * d).astype(BF16), (kf * d).astype(BF16))
    o = o + _dot(a.astype(BF16), v)
    kr = (kf * jnp.exp(end - cum)).astype(BF16)
    st_new = st * jnp.exp(end) + _dot(v.astype(F32).T.astype(BF16), kr)
    return o, st_new


def _hgrn_state_kernel(kf_ref, gf_ref, kb_ref, gb_ref, v_ref, tri_ref, sf_ref, sb_ref):
    v = v_ref[...]
    vt = v.astype(F32).T.astype(BF16)
    n = v.shape[0]
    tri = tri_ref[...]
    for d, (k_ref, g_ref, s_ref) in enumerate(((kf_ref, gf_ref, sf_ref), (kb_ref, gb_ref, sb_ref))):
        g_hi, g_lo = _split_bf16(g_ref[...])
        cum = _dot(tri[d], g_hi) + _dot(tri[d], g_lo)
        end = cum[n - 1:n] if d == 0 else cum[0:1]
        kr = (k_ref[...].astype(F32) * jnp.exp(end - cum)).astype(BF16)
        s_ref[...] = _dot(vt, kr)


def _hgrn_state(kf, gf, kb, gb, v):
    b, n, w = v.shape
    heads = w // HGRN_DK
    idx = np.arange(n)
    tri = jnp.asarray(np.stack([idx[None, :] <= idx[:, None], idx[None, :] >= idx[:, None]]), BF16)
    tile = pl.BlockSpec((None, n, HGRN_DK), lambda i, hh: (i, 0, hh))
    st = pl.BlockSpec((None, None, HGRN_DK, HGRN_DK), lambda i, hh: (i, hh, 0, 0))
    shape = jax.ShapeDtypeStruct((b, heads, HGRN_DK, HGRN_DK), F32)
    return pl.pallas_call(
        _hgrn_state_kernel,
        grid=(b, heads),
        in_specs=[tile] * 5 + [pl.BlockSpec((2, n, n), lambda i, hh: (0, 0, 0))],
        out_specs=[st, st],
        out_shape=[shape, shape],
        compiler_params=_params("parallel", "parallel"),
        name="hgrn_state",
    )(kf, gf, kb, gb, v, tri)


def _hgrn_scan_kernel(qf_ref, vf_ref, kf_ref, gf_ref, qb_ref, vb_ref, kb_ref, gb_ref,
                      s0f_ref, s0b_ref, mf_ref, mb_ref, mkf_ref, mkb_ref,
                      of_ref, ob_ref, stf_ref, stb_ref, *, chunk):
    @pl.when(pl.program_id(2) == 0)
    def _():
        stf_ref[...] = s0f_ref[...]
        stb_ref[...] = s0b_ref[...]

    n_chunks = qf_ref.shape[0] // chunk
    st_f, st_b = stf_ref[...], stb_ref[...]
    mf, mb = mf_ref[...], mb_ref[...]
    for ci in range(n_chunks):
        sl = slice(ci * chunk, (ci + 1) * chunk)
        o, st_f = _hgrn_chunk(qf_ref[sl, :], kf_ref[sl, :], gf_ref[sl, :], vf_ref[sl, :],
                              st_f, mf, mkf_ref, False)
        of_ref[sl, :] = o
        rc = n_chunks - 1 - ci
        sl = slice(rc * chunk, (rc + 1) * chunk)
        o, st_b = _hgrn_chunk(qb_ref[sl, :], kb_ref[sl, :], gb_ref[sl, :], vb_ref[sl, :],
                              st_b, mb, mkb_ref, True)
        ob_ref[sl, :] = o
    stf_ref[...] = st_f
    stb_ref[...] = st_b


def _hgrn_scan(q, v, kf, gf, kb, gb, s0f, s0b, step, chunk):
    b, n, w = q.shape
    heads = w // HGRN_DK
    nt = n // step
    mf, mkf = _chunk_tables(chunk, False)
    mb, mkb = _chunk_tables(chunk, True)
    fwd = pl.BlockSpec((None, step, HGRN_DK), lambda i, hh, t: (i, t, hh))
    bwd = pl.BlockSpec((None, step, HGRN_DK), lambda i, hh, t: (i, nt - 1 - t, hh))
    st = pl.BlockSpec((None, None, HGRN_DK, HGRN_DK), lambda i, hh, t: (i, hh, 0, 0))
    c2 = pl.BlockSpec(mf.shape, lambda i, hh, t: (0, 0))
    c3 = pl.BlockSpec(mkf.shape, lambda i, hh, t: (0, 0, 0))
    out = jax.ShapeDtypeStruct((b, n, w), F32)
    return pl.pallas_call(
        functools.partial(_hgrn_scan_kernel, chunk=chunk),
        grid=(b, heads, nt),
        in_specs=[fwd] * 4 + [bwd] * 4 + [st, st, c2, c2, c3, c3],
        out_specs=[fwd, bwd],
        out_shape=[out, out],
        scratch_shapes=[pltpu.VMEM((HGRN_DK, HGRN_DK), F32)] * 2,
        compiler_params=_params("parallel", "parallel", "arbitrary"),
        name="hgrn_scan",
    )(q, v, kf, gf, q, v, kb, gb, s0f, s0b, mf, mb, mkf, mkb)


def kernel(x, c, ctx, c_ctx, ada_w, ada_b, norm_mix_w, norm_ffn_w, attn_w_qkv, attn_q_norm,
           attn_k_norm, attn_w_o, hgrn_w_in, hgrn_lb_logits, hgrn_out_norm, hgrn_w_o,
           ffn_w_in, ffn_w_out, final_norm_w):
    b, n, d = x.shape
    m = ctx.shape[1]
    depth = ada_w.shape[0]
    tm = min(512, n)

    rows = -(-(b + 1) // 8) * 8
    cc = jnp.zeros((rows, d), F32).at[:b].set(c).at[b].set(c_ctx)
    mod = _adaln(cc, ada_w, ada_b)

    def mods(i, lat):
        parts = []
        for s in range(6):
            blk = mod[i, :, s * d:(s + 1) * d]
            if lat:
                parts.append(blk[:b, None, :])
            else:
                parts.append(jnp.broadcast_to(blk[b][None, None, :], (b, 1, d)))
        return parts

    h, hc = x, ctx
    for i in range(depth):
        last = i == depth - 1
        j = i // 2
        sh1, sc1, gt1, sh2, sc2, gt2 = mods(i, True)
        csh1, csc1, cgt1, csh2, csc2, cgt2 = mods(i, False)
        nmw = norm_mix_w[i][None, :]
        nfw = norm_ffn_w[i][None, :]
        wi = ffn_w_in[i].astype(BF16)
        wout = ffn_w_out[i].astype(BF16)
        fin = final_norm_w[None, :] if last else None
        if i % 2 == 0:
            w_qkv = attn_w_qkv[j].astype(BF16)
            wo = attn_w_o[j].astype(BF16)
            qn = jnp.tile(attn_q_norm[j], LANES // HEAD_DIM)[None, :]
            kn = jnp.tile(attn_k_norm[j], LANES // HEAD_DIM)[None, :]
            q_l, k_l, v_l = _qkv(h, nmw, sh1, sc1, w_qkv, qn, kn, _rope_tables(n), tm)
            q_c, k_c, v_c = _qkv(hc, nmw, csh1, csc1, w_qkv, qn, kn, None, m)
            k_all = jnp.concatenate([k_c, k_l], axis=2)
            v_all = jnp.concatenate([v_c, v_l], axis=3)
            o_l = _attention(q_l, k_all, v_all, 256, 768 if (m + n) % 768 == 0 else m)
            h = _post(h, [o_l], wo, gt1, nfw, sh2, sc2, gt2, wi, wout, fin, "attn", tm)
            if not last:
                o_c = _attention(q_c, k_c, v_c, m, m)
                hc = _post(hc, [o_c], wo, cgt1, nfw, csh2, csc2, cgt2, wi, wout, None, "attn", m)
        else:
            w_in = hgrn_w_in[j].astype(BF16)
            wo = hgrn_w_o[j].astype(BF16)
            on = hgrn_out_norm[j][None, :]
            q, gate, gf, kf, gb, kb, v = _hgrn_in(h, nmw, sh1, sc1, w_in, hgrn_lb_logits, i, tm)
            cq, cgate, cgf, ckf, cgb, ckb, cv = _hgrn_in(hc, nmw, csh1, csc1, w_in,
                                                         hgrn_lb_logits, i, m)
            if last:
                s0f, s0b = _hgrn_state(ckf, cgf, ckb, cgb, cv)
            else:
                zero = jnp.zeros((b, HGRN_HEADS, HGRN_DK, HGRN_DK), F32)
                oc_f, oc_b = _hgrn_scan(cq, cv, ckf, cgf, ckb, cgb, zero, zero, m, HGRN_CHUNK)
                s0f, s0b = _hgrn_state(ckf, cgf, ckb, cgb, cv)
                hc = _post(hc, [oc_f, oc_b, cgate, on], wo, cgt1, nfw, csh2, csc2, cgt2, wi, wout,
                           None, "hgrn", m)
            o_f, o_b = _hgrn_scan(q, v, kf, gf, kb, gb, s0f, s0b, min(HGRN_STEP, n), HGRN_CHUNK)
            h = _post(h, [o_f, o_b, gate, on], wo, gt1, nfw, sh2, sc2, gt2, wi, wout, fin, "hgrn", tm)
    return h
```

```python
import functools

import numpy as np
import jax
import jax.numpy as jnp
from jax import lax
from jax.experimental import pallas as pl
from jax.experimental.pallas import tpu as pltpu

F32 = jnp.float32
BF16 = jnp.bfloat16

NORM_EPS = 1e-6
GRID_W = 64
ATTN_HEADS = 16
ATTN_KV_HEADS = 4
HEAD_DIM = 64
ATTN_GROUP = ATTN_HEADS // ATTN_KV_HEADS
ROPE_THETA = 10000.0
HGRN_HEADS = 8
HGRN_DK = 128

LANES = 128
SUBLANES = 8
VMEM_LIMIT = 56 * 1024 * 1024

HGRN_CHUNK = 128
HGRN_STEP = 1024


def _params(*sem):
    return pltpu.CompilerParams(dimension_semantics=sem, vmem_limit_bytes=VMEM_LIMIT)


def _dot(a, b):
    return jnp.dot(a, b, preferred_element_type=F32)


def _dot_nt(a, b):
    return lax.dot_general(a, b, (((1,), (1,)), ((), ())), preferred_element_type=F32)


def _sigmoid(x):
    return 1.0 / (1.0 + jnp.exp(-x))


def _rms(x):
    return x * lax.rsqrt(jnp.mean(x * x, axis=-1, keepdims=True) + NORM_EPS)


def _adaln_kernel(c_ref, w_ref, b_ref, o_ref):
    c = c_ref[...]
    a = (c * _sigmoid(c)).astype(BF16)
    o_ref[...] = _dot(a, w_ref[...].astype(BF16)) + b_ref[...]


def _adaln(cc, ada_w, ada_b):
    depth, d, n = ada_w.shape
    rows = cc.shape[0]
    tn = 1536
    return pl.pallas_call(
        _adaln_kernel,
        grid=(depth, n // tn),
        in_specs=[
            pl.BlockSpec((rows, d), lambda i, j: (0, 0)),
            pl.BlockSpec((None, d, tn), lambda i, j: (i, 0, j)),
            pl.BlockSpec((None, 1, tn), lambda i, j: (i, 0, j)),
        ],
        out_specs=pl.BlockSpec((None, rows, tn), lambda i, j: (i, 0, j)),
        out_shape=jax.ShapeDtypeStruct((depth, rows, n), F32),
        compiler_params=_params("parallel", "parallel"),
        name="adaln",
    )(cc, ada_w, ada_b.reshape(depth, 1, n))


def _swap16(x):
    lane = lax.broadcasted_iota(jnp.int32, x.shape, 1)
    up = pltpu.roll(x, LANES - 16, axis=1)
    down = pltpu.roll(x, 16, axis=1)
    return jnp.where((lane % 32) < 16, up, down)


def _head_norm_rope(x, w, cos, sin):
    lane = lax.broadcasted_iota(jnp.int32, x.shape, 1)
    lo = lane < HEAD_DIM
    sq = x * x
    s_lo = jnp.sum(jnp.where(lo, sq, 0.0), axis=-1, keepdims=True)
    s_hi = jnp.sum(jnp.where(lo, 0.0, sq), axis=-1, keepdims=True)
    ms = jnp.where(lo, s_lo, s_hi) * (1.0 / HEAD_DIM)
    xn = x * lax.rsqrt(ms + NORM_EPS) * w
    if cos is None:
        return xn
    return xn * cos + _swap16(xn) * sin


def _qkv_kernel(*refs, rope, q_scale):
    if rope:
        (x_ref, nw_ref, sh_ref, sc_ref, w_ref, qn_ref, kn_ref, cos_ref, sin_ref,
         qt_ref, k_ref, vt_ref) = refs
        cos, sin = cos_ref[...], sin_ref[...]
    else:
        x_ref, nw_ref, sh_ref, sc_ref, w_ref, qn_ref, kn_ref, qt_ref, k_ref, vt_ref = refs
        cos = sin = None
    x = x_ref[...]
    hn = (_rms(x) * nw_ref[...]) * (1.0 + sc_ref[...]) + sh_ref[...]
    z = _dot(hn.astype(BF16), w_ref[...])
    tm = z.shape[0]
    qw = qt_ref.shape[0]
    kw = k_ref.shape[0] * HEAD_DIM
    for s in range(qw // LANES):
        xs = z[:, s * LANES:(s + 1) * LANES]
        qn = _head_norm_rope(xs, qn_ref[...], cos, sin) * q_scale
        qt_ref[s * LANES:(s + 1) * LANES, :] = qn.T.astype(BF16)
    for s in range(kw // LANES):
        xs = z[:, qw + s * LANES:qw + (s + 1) * LANES]
        kn = _head_norm_rope(xs, kn_ref[...], cos, sin)
        k_ref[2 * s] = kn[:, :HEAD_DIM].astype(BF16)
        k_ref[2 * s + 1] = kn[:, HEAD_DIM:].astype(BF16)
    row = lax.broadcasted_iota(jnp.int32, (LANES - HEAD_DIM, tm), 0)
    pad = jnp.where(row == 0, 1.0, 0.0).astype(BF16)
    for s in range(kw // LANES):
        vt = z[:, qw + kw + s * LANES:qw + kw + (s + 1) * LANES].T.astype(BF16)
        for half in range(2):
            vt_ref[2 * s + half, :HEAD_DIM, :] = vt[half * HEAD_DIM:(half + 1) * HEAD_DIM]
            vt_ref[2 * s + half, HEAD_DIM:, :] = pad


def _qkv(h, nw, sh, sc, w_qkv, qn, kn, rope_tabs, tm):
    b, t, d = h.shape
    n = w_qkv.shape[1]
    qw = ATTN_HEADS * HEAD_DIM
    kw = ATTN_KV_HEADS * HEAD_DIM
    rope = rope_tabs is not None
    row = lambda i, j: (i, 0, 0)
    in_specs = [
        pl.BlockSpec((None, tm, d), lambda i, j: (i, j, 0)),
        pl.BlockSpec((1, d), lambda i, j: (0, 0)),
        pl.BlockSpec((None, 1, d), row),
        pl.BlockSpec((None, 1, d), row),
        pl.BlockSpec((d, n), lambda i, j: (0, 0)),
        pl.BlockSpec((1, LANES), lambda i, j: (0, 0)),
        pl.BlockSpec((1, LANES), lambda i, j: (0, 0)),
    ]
    args = [h, nw, sh, sc, w_qkv, qn, kn]
    if rope:
        in_specs += [pl.BlockSpec((tm, LANES), lambda i, j: (j, 0))] * 2
        args += list(rope_tabs)
    return pl.pallas_call(
        functools.partial(_qkv_kernel, rope=rope, q_scale=HEAD_DIM ** -0.5 * np.log2(np.e)),
        grid=(b, t // tm),
        in_specs=in_specs,
        out_specs=[
            pl.BlockSpec((None, qw, tm), lambda i, j: (i, 0, j)),
            pl.BlockSpec((None, ATTN_KV_HEADS, tm, HEAD_DIM), lambda i, j: (i, 0, j, 0)),
            pl.BlockSpec((None, ATTN_KV_HEADS, LANES, tm), lambda i, j: (i, 0, 0, j)),
        ],
        out_shape=[
            jax.ShapeDtypeStruct((b, qw, t), BF16),
            jax.ShapeDtypeStruct((b, ATTN_KV_HEADS, t, HEAD_DIM), BF16),
            jax.ShapeDtypeStruct((b, ATTN_KV_HEADS, LANES, t), BF16),
        ],
        compiler_params=_params("parallel", "parallel"),
        name="qkv_rope" if rope else "qkv_ctx",
    )(*args)


def _rope_tables(n_tokens):
    axis_dim = HEAD_DIM // 2
    pos = np.arange(n_tokens)
    inv_freq = ROPE_THETA ** (-(np.arange(axis_dim // 2, dtype=np.float32) * 2.0 / axis_dim))
    ang_r = (pos // GRID_W).astype(np.float32)[:, None] * inv_freq
    ang_c = (pos % GRID_W).astype(np.float32)[:, None] * inv_freq
    cos = np.concatenate([np.cos(ang_r)] * 2 + [np.cos(ang_c)] * 2, axis=1)
    sin = np.concatenate([-np.sin(ang_r), np.sin(ang_r), -np.sin(ang_c), np.sin(ang_c)], axis=1)
    reps = LANES // HEAD_DIM
    return (jnp.asarray(np.tile(cos, (1, reps)), F32), jnp.asarray(np.tile(sin, (1, reps)), F32))


def _attn_kernel(qt_ref, k_ref, vt_ref, o_ref, s0_ref, s1_ref, p0_ref, p1_ref, *, tk):
    tq = qt_ref.shape[1]
    nk = k_ref.shape[0] // tk
    s_bufs, p_bufs = (s0_ref, s1_ref), (p0_ref, p1_ref)
    qt = qt_ref[...]
    qs = jnp.concatenate([qt[g * HEAD_DIM:(g + 1) * HEAD_DIM] for g in range(ATTN_GROUP)], axis=1)
    cols = qs.shape[1]

    def scores(j, par):
        start = pl.multiple_of(j * tk, tk)
        s_bufs[par][...] = _dot(k_ref[pl.ds(start, tk), :], qs)

    def softmax(par, m):
        s = s_bufs[par][...]
        m_new = jnp.maximum(m, jnp.max(s, axis=0, keepdims=True))
        p_bufs[par][...] = jnp.exp2(s - m_new).astype(BF16)
        return m_new, jnp.exp2(m - m_new)

    def weighted(j, par, alpha, acc):
        start = pl.multiple_of(j * tk, tk)
        return alpha * acc + _dot(vt_ref[:, pl.ds(start, tk)], p_bufs[par][...])

    def step(k, par, carry):
        m, alpha, acc = carry
        scores(k + 2, par)
        m, alpha_next = softmax(1 - par, m)
        return m, alpha_next, weighted(k, par, alpha, acc)

    scores(0, 0)
    m, alpha = softmax(0, jnp.full((1, cols), -jnp.inf, F32))
    acc = jnp.zeros((LANES, cols), F32)
    if nk > 1:
        scores(1, 1)
        carry = (m, alpha, acc)
        for k in range(nk - 2):
            carry = step(k, k % 2, carry)
        m, alpha, acc = carry
        m, alpha_last = softmax((nk - 1) % 2, m)
        acc = weighted(nk - 2, (nk - 2) % 2, alpha, acc)
        alpha = alpha_last
    acc = weighted(nk - 1, (nk - 1) % 2, alpha, acc)
    o = (acc / acc[HEAD_DIM:HEAD_DIM + 1]).T
    o_ref[...] = jnp.concatenate(
        [o[g * tq:(g + 1) * tq, :HEAD_DIM] for g in range(ATTN_GROUP)], axis=1).astype(BF16)


def _attention(qt, k, vt, tq, tk):
    b, qw, t = qt.shape
    s = k.shape[2]
    gw = ATTN_GROUP * HEAD_DIM
    return pl.pallas_call(
        functools.partial(_attn_kernel, tk=tk),
        grid=(b, ATTN_KV_HEADS, t // tq),
        in_specs=[
            pl.BlockSpec((None, gw, tq), lambda i, k, j: (i, k, j)),
            pl.BlockSpec((None, None, s, HEAD_DIM), lambda i, k, j: (i, k, 0, 0)),
            pl.BlockSpec((None, None, LANES, s), lambda i, k, j: (i, k, 0, 0)),
        ],
        out_specs=pl.BlockSpec((None, tq, gw), lambda i, k, j: (i, j, k)),
        out_shape=jax.ShapeDtypeStruct((b, t, qw), BF16),
        scratch_shapes=[pltpu.VMEM((tk, ATTN_GROUP * tq), F32)] * 2
        + [pltpu.VMEM((tk, ATTN_GROUP * tq), BF16)] * 2,
        compiler_params=_params("parallel", "parallel", "parallel"),
        name="attention",
    )(qt, k, vt)


def _post_kernel(*refs, mode, final, ffn_splits):
    if mode == "attn":
        (h_ref, y_ref, wo_ref, g1_ref, nw_ref, sh_ref, sc_ref, g2_ref, wi_ref, wout_ref) = refs[:10]
        rest = refs[10:]
        y_in = y_ref[...]
    else:
        (h_ref, of_ref, ob_ref, gate_ref, on_ref, wo_ref, g1_ref, nw_ref, sh_ref, sc_ref, g2_ref,
         wi_ref, wout_ref) = refs[:13]
        rest = refs[13:]
        o = of_ref[...] + ob_ref[...]
        gate = gate_ref[...]
        parts = []
        for hh in range(o.shape[1] // HGRN_DK):
            sl = slice(hh * HGRN_DK, (hh + 1) * HGRN_DK)
            parts.append(_rms(o[:, sl]) * on_ref[:, sl] * _sigmoid(gate[:, sl]))
        y_in = jnp.concatenate(parts, axis=1).astype(BF16)
    if final:
        fw_ref, out_ref = rest
    else:
        (out_ref,) = rest
    h1 = h_ref[...] + g1_ref[...] * _dot(y_in, wo_ref[...])
    hn = ((_rms(h1) * nw_ref[...]) * (1.0 + sc_ref[...]) + sh_ref[...]).astype(BF16)
    hidden = wout_ref.shape[0]
    acc = None
    off = 0
    for width in ffn_splits:
        a = _dot(hn, wi_ref[:, off:off + width])
        u = _dot(hn, wi_ref[:, hidden + off:hidden + off + width])
        gu = (a * _sigmoid(a) * u).astype(BF16)
        part = _dot(gu, wout_ref[off:off + width, :])
        acc = part if acc is None else acc + part
        off += width
    h2 = h1 + g2_ref[...] * acc
    if final:
        h2 = _rms(h2) * fw_ref[...]
    out_ref[...] = h2


def _post(h, mixer_in, wo, g1, nw, sh, sc, g2, wi, wout, final_w, mode, tm):
    b, t, d = h.shape
    hidden = wout.shape[0]
    tile = pl.BlockSpec((None, tm, d), lambda i, j: (i, j, 0))
    row = pl.BlockSpec((None, 1, d), lambda i, j: (i, 0, 0))
    const = lambda shape: pl.BlockSpec(shape, lambda i, j: (0,) * len(shape),
                                       pipeline_mode=pl.Buffered(1))
    if mode == "attn":
        mix_specs = [pl.BlockSpec((None, tm, mixer_in[0].shape[-1]), lambda i, j: (i, j, 0))]
    else:
        mix_specs = [tile, tile, tile, const((1, d))]
    in_specs = [tile] + mix_specs + [const(wo.shape), row, const((1, d)), row, row, row,
                                     const(wi.shape), const(wout.shape)]
    args = [h] + list(mixer_in) + [wo, g1, nw, sh, sc, g2, wi, wout]
    final = final_w is not None
    if final:
        in_specs.append(const((1, d)))
        args.append(final_w)
    splits = (1536, hidden - 1536) if hidden > 1536 else (hidden,)
    return pl.pallas_call(
        functools.partial(_post_kernel, mode=mode, final=final, ffn_splits=splits),
        grid=(b, t // tm),
        in_specs=in_specs,
        out_specs=tile,
        out_shape=jax.ShapeDtypeStruct((b, t, d), F32),
        compiler_params=_params("parallel", "parallel"),
        name="post_" + mode + ("_final" if final else ""),
    )(*args)


def _hgrn_in_kernel(x_ref, nw_ref, sh_ref, sc_ref, w_ref, lbl_ref,
                    q_ref, gate_ref, gf_ref, kf_ref, gb_ref, kb_ref, v_ref, *, layer):
    x = x_ref[...]
    hn = ((_rms(x) * nw_ref[...]) * (1.0 + sc_ref[...]) + sh_ref[...]).astype(BF16)
    wd = q_ref.shape[-1]
    lg = lbl_ref[...]
    e = jnp.exp(lg - jnp.max(lg, axis=0, keepdims=True))
    p = e / jnp.sum(e, axis=0, keepdims=True)
    lb = jnp.sum(p[1:layer + 1], axis=0, keepdims=True) if layer > 0 else jnp.zeros_like(p[0:1])

    def proj(i):
        return _dot(hn, w_ref[:, i * wd:(i + 1) * wd])

    q_ref[...] = proj(0).astype(BF16)
    gate_ref[...] = proj(1)
    for i, (g_ref, k_ref) in ((2, (gf_ref, kf_ref)), (3, (gb_ref, kb_ref))):
        f = lb + (1.0 - lb) * _sigmoid(proj(i))
        g_ref[...] = jnp.log(f)
        k_ref[...] = (1.0 - f).astype(BF16)
    v_ref[...] = proj(4).astype(BF16)


def _hgrn_in(h, nw, sh, sc, w_in, lb_logits, layer, tm):
    b, t, d = h.shape
    wd = w_in.shape[1] // 5
    tile = lambda: pl.BlockSpec((None, tm, wd), lambda i, j: (i, j, 0))
    row = pl.BlockSpec((None, 1, d), lambda i, j: (i, 0, 0))
    depth = lb_logits.shape[0]
    dts = (BF16, F32, F32, BF16, F32, BF16, BF16)
    return pl.pallas_call(
        functools.partial(_hgrn_in_kernel, layer=layer),
        grid=(b, t // tm),
        in_specs=[
            pl.BlockSpec((None, tm, d), lambda i, j: (i, j, 0)),
            pl.BlockSpec((1, d), lambda i, j: (0, 0)),
            row, row,
            pl.BlockSpec(w_in.shape, lambda i, j: (0, 0), pipeline_mode=pl.Buffered(1)),
            pl.BlockSpec((depth, wd), lambda i, j: (0, 0)),
        ],
        out_specs=[tile() for _ in dts],
        out_shape=[jax.ShapeDtypeStruct((b, t, wd), dt) for dt in dts],
        compiler_params=_params("parallel", "parallel"),
        name="hgrn_in",
    )(h, nw, sh, sc, w_in, lb_logits)


def _chunk_tables(c, rev):
    levels = int(np.log2(c))
    idx = np.arange(c)
    t, j = idx[:, None], idx[None, :]
    mats = [(j <= t)]
    masks = [np.eye(c, dtype=bool)]
    for lv in range(1, levels + 1):
        blk = 2 ** lv
        mid = (idx // blk) * blk + blk // 2
        upper = (idx % blk) >= blk // 2
        m_up = upper[:, None] & (j >= mid[:, None]) & (j <= t)
        m_lo = (~upper)[:, None] & (j > t) & (j <= mid[:, None] - 1)
        if blk <= SUBLANES:
            mats.append(m_up | m_lo)
        same = (idx[:, None] // blk) == (idx[None, :] // blk)
        masks.append(same & upper[:, None] & (~upper)[None, :])
    mats = np.stack(mats).astype(np.float32)
    masks = np.stack(masks).astype(np.float32)
    if rev:
        mats = mats[:, ::-1, ::-1]
        masks = masks[:, ::-1, ::-1]
    return (jnp.asarray(mats.reshape(-1, c), BF16), jnp.asarray(masks, F32))


def _split_bf16(g):
    hi = g.astype(BF16)
    return hi, (g - hi.astype(F32)).astype(BF16)


def _level_exponents(cum, blk, rev):
    half = blk // 2
    parts = []
    for b0 in range(0, cum.shape[0], blk):
        lo, up = cum[b0:b0 + half], cum[b0 + half:b0 + blk]
        if rev:
            ref = cum[b0 + half:b0 + half + 1]
            parts += [lo - ref, ref - up]
        else:
            ref = cum[b0 + half - 1:b0 + half]
            parts += [ref - lo, up - ref]
    return jnp.concatenate(parts, axis=0)


def _hgrn_tables(g, mst):
    g_hi, g_lo = _split_bf16(g)
    return _dot(mst, g_hi) + _dot(mst, g_lo)


def _hgrn_intra(q, k, v, e, masks, rev):
    c = q.shape[0]
    cum = e[0:c]
    end = cum[0:1] if rev else cum[c - 1:c]
    qf, kf = q.astype(F32), k.astype(F32)
    a = masks[0] * _dot_nt(q, k)
    for lv in range(1, masks.shape[0]):
        blk = 2 ** lv
        ex = e[lv * c:(lv + 1) * c] if blk <= SUBLANES else _level_exponents(cum, blk, rev)
        d = jnp.exp(ex)
        a = a + masks[lv] * _dot_nt((qf * d).astype(BF16), (kf * d).astype(BF16))
    o_intra = _dot(a.astype(BF16), v)
    q_in = (qf * jnp.exp(cum)).astype(BF16)
    kr = (kf * jnp.exp(end - cum)).astype(BF16)
    st_add = _dot(v.astype(F32).T.astype(BF16), kr)
    return o_intra, q_in, jnp.exp(end), st_add


def _hgrn_state_kernel(kf_ref, gf_ref, kb_ref, gb_ref, v_ref, tri_ref, sf_ref, sb_ref):
    v = v_ref[...]
    vt = v.astype(F32).T.astype(BF16)
    n = v.shape[0]
    tri = tri_ref[...]
    for d, (k_ref, g_ref, s_ref) in enumerate(((kf_ref, gf_ref, sf_ref), (kb_ref, gb_ref, sb_ref))):
        g_hi, g_lo = _split_bf16(g_ref[...])
        cum = _dot(tri[d], g_hi) + _dot(tri[d], g_lo)
        end = cum[n - 1:n] if d == 0 else cum[0:1]
        kr = (k_ref[...].astype(F32) * jnp.exp(end - cum)).astype(BF16)
        s_ref[...] = _dot(vt, kr)


def _hgrn_state(kf, gf, kb, gb, v):
    b, n, w = v.shape
    heads = w // HGRN_DK
    idx = np.arange(n)
    tri = jnp.asarray(np.stack([idx[None, :] <= idx[:, None], idx[None, :] >= idx[:, None]]), BF16)
    tile = pl.BlockSpec((None, n, HGRN_DK), lambda i, hh: (i, 0, hh))
    st = pl.BlockSpec((None, None, HGRN_DK, HGRN_DK), lambda i, hh: (i, hh, 0, 0))
    shape = jax.ShapeDtypeStruct((b, heads, HGRN_DK, HGRN_DK), F32)
    return pl.pallas_call(
        _hgrn_state_kernel,
        grid=(b, heads),
        in_specs=[tile] * 5 + [pl.BlockSpec((2, n, n), lambda i, hh: (0, 0, 0))],
        out_specs=[st, st],
        out_shape=[shape, shape],
        compiler_params=_params("parallel", "parallel"),
        name="hgrn_state",
    )(kf, gf, kb, gb, v, tri)


def _hgrn_scan_kernel(qf_ref, vf_ref, kf_ref, gf_ref, qb_ref, vb_ref, kb_ref, gb_ref,
                      s0f_ref, s0b_ref, mf_ref, mb_ref, mkf_ref, mkb_ref,
                      of_ref, ob_ref, stf_ref, stb_ref, *, chunk):
    @pl.when(pl.program_id(2) == 0)
    def _():
        stf_ref[...] = s0f_ref[...]
        stb_ref[...] = s0b_ref[...]

    n_chunks = qf_ref.shape[0] // chunk
    mf, mb = mf_ref[...], mb_ref[...]
    bodies = []
    for ci in range(n_chunks):
        bodies.append((slice(ci * chunk, (ci + 1) * chunk), False))
        rc = n_chunks - 1 - ci
        bodies.append((slice(rc * chunk, (rc + 1) * chunk), True))
    refs = {False: (qf_ref, kf_ref, gf_ref, vf_ref, mf, mkf_ref, of_ref),
            True: (qb_ref, kb_ref, gb_ref, vb_ref, mb, mkb_ref, ob_ref)}
    tables = [_hgrn_tables(refs[rev][2][sl, :], refs[rev][4]) for sl, rev in bodies]
    parts = []
    for (sl, rev), e in zip(bodies, tables):
        q_ref, k_ref, _, v_ref, _, masks, _ = refs[rev]
        parts.append(_hgrn_intra(q_ref[sl, :], k_ref[sl, :], v_ref[sl, :], e, masks, rev))
    st = {False: stf_ref[...], True: stb_ref[...]}
    for (sl, rev), (o_intra, q_in, dec_end, st_add) in zip(bodies, parts):
        refs[rev][6][sl, :] = o_intra + _dot_nt(q_in, st[rev].astype(BF16))
        st[rev] = st[rev] * dec_end + st_add
    stf_ref[...] = st[False]
    stb_ref[...] = st[True]


def _hgrn_scan(q, v, kf, gf, kb, gb, s0f, s0b, step, chunk):
    b, n, w = q.shape
    heads = w // HGRN_DK
    nt = n // step
    mf, mkf = _chunk_tables(chunk, False)
    mb, mkb = _chunk_tables(chunk, True)
    fwd = pl.BlockSpec((None, step, HGRN_DK), lambda i, hh, t: (i, t, hh))
    bwd = pl.BlockSpec((None, step, HGRN_DK), lambda i, hh, t: (i, nt - 1 - t, hh))
    st = pl.BlockSpec((None, None, HGRN_DK, HGRN_DK), lambda i, hh, t: (i, hh, 0, 0))
    c2 = pl.BlockSpec(mf.shape, lambda i, hh, t: (0, 0))
    c3 = pl.BlockSpec(mkf.shape, lambda i, hh, t: (0, 0, 0))
    out = jax.ShapeDtypeStruct((b, n, w), F32)
    return pl.pallas_call(
        functools.partial(_hgrn_scan_kernel, chunk=chunk),
        grid=(b, heads, nt),
        in_specs=[fwd] * 4 + [bwd] * 4 + [st, st, c2, c2, c3, c3],
        out_specs=[fwd, bwd],
        out_shape=[out, out],
        scratch_shapes=[pltpu.VMEM((HGRN_DK, HGRN_DK), F32)] * 2,
        compiler_params=_params("parallel", "parallel", "arbitrary"),
        name="hgrn_scan",
    )(q, v, kf, gf, q, v, kb, gb, s0f, s0b, mf, mb, mkf, mkb)


def kernel(x, c, ctx, c_ctx, ada_w, ada_b, norm_mix_w, norm_ffn_w, attn_w_qkv, attn_q_norm,
           attn_k_norm, attn_w_o, hgrn_w_in, hgrn_lb_logits, hgrn_out_norm, hgrn_w_o,
           ffn_w_in, ffn_w_out, final_norm_w):
    b, n, d = x.shape
    m = ctx.shape[1]
    depth = ada_w.shape[0]
    tm = min(512, n)

    rows = -(-(b + 1) // 8) * 8
    cc = jnp.zeros((rows, d), F32).at[:b].set(c).at[b].set(c_ctx)
    mod = _adaln(cc, ada_w, ada_b)

    def mods(i, lat):
        parts = []
        for s in range(6):
            blk = mod[i, :, s * d:(s + 1) * d]
            if lat:
                parts.append(blk[:b, None, :])
            else:
                parts.append(jnp.broadcast_to(blk[b][None, None, :], (b, 1, d)))
        return parts

    h, hc = x, ctx
    for i in range(depth):
        last = i == depth - 1
        j = i // 2
        sh1, sc1, gt1, sh2, sc2, gt2 = mods(i, True)
        csh1, csc1, cgt1, csh2, csc2, cgt2 = mods(i, False)
        nmw = norm_mix_w[i][None, :]
        nfw = norm_ffn_w[i][None, :]
        wi = ffn_w_in[i].astype(BF16)
        wout = ffn_w_out[i].astype(BF16)
        fin = final_norm_w[None, :] if last else None
        if i % 2 == 0:
            w_qkv = attn_w_qkv[j].astype(BF16)
            wo = attn_w_o[j].astype(BF16)
            qn = jnp.tile(attn_q_norm[j], LANES // HEAD_DIM)[None, :]
            kn = jnp.tile(attn_k_norm[j], LANES // HEAD_DIM)[None, :]
            q_l, k_l, v_l = _qkv(h, nmw, sh1, sc1, w_qkv, qn, kn, _rope_tables(n), tm)
            q_c, k_c, v_c = _qkv(hc, nmw, csh1, csc1, w_qkv, qn, kn, None, m)
            k_all = jnp.concatenate([k_c, k_l], axis=2)
            v_all = jnp.concatenate([v_c, v_l], axis=3)
            o_l = _attention(q_l, k_all, v_all, 256, 768 if (m + n) % 768 == 0 else m)
            h = _post(h, [o_l], wo, gt1, nfw, sh2, sc2, gt2, wi, wout, fin, "attn", tm)
            if not last:
                o_c = _attention(q_c, k_c, v_c, m, m)
                hc = _post(hc, [o_c], wo, cgt1, nfw, csh2, csc2, cgt2, wi, wout, None, "attn", m)
        else:
            w_in = hgrn_w_in[j].astype(BF16)
            wo = hgrn_w_o[j].astype(BF16)
            on = hgrn_out_norm[j][None, :]
            q, gate, gf, kf, gb, kb, v = _hgrn_in(h, nmw, sh1, sc1, w_in, hgrn_lb_logits, i, tm)
            cq, cgate, cgf, ckf, cgb, ckb, cv = _hgrn_in(hc, nmw, csh1, csc1, w_in,
                                                         hgrn_lb_logits, i, m)
            if last:
                s0f, s0b = _hgrn_state(ckf, cgf, ckb, cgb, cv)
            else:
                zero = jnp.zeros((b, HGRN_HEADS, HGRN_DK, HGRN_DK), F32)
                oc_f, oc_b = _hgrn_scan(cq, cv, ckf, cgf, ckb, cgb, zero, zero, m, HGRN_CHUNK)
                s0f, s0b = _hgrn_state(ckf, cgf, ckb, cgb, cv)
                hc = _post(hc, [oc_f, oc_b, cgate, on], wo, cgt1, nfw, csh2, csc2, cgt2, wi, wout,
                           None, "hgrn", m)
            o_f, o_b = _hgrn_scan(q, v, kf, gf, kb, gb, s0f, s0b, min(HGRN_STEP, n), HGRN_CHUNK)
            h = _post(h, [o_f, o_b, gate, on], wo, gt1, nfw, sh2, sc2, gt2, wi, wout, fin, "hgrn", tm)
    return h
```

```python
import functools

import numpy as np
import jax
import jax.numpy as jnp
from jax import lax
from jax.experimental import pallas as pl
from jax.experimental.pallas import tpu as pltpu

F32 = jnp.float32
BF16 = jnp.bfloat16

NORM_EPS = 1e-6
GRID_W = 64
ATTN_HEADS = 16
ATTN_KV_HEADS = 4
HEAD_DIM = 64
ATTN_GROUP = ATTN_HEADS // ATTN_KV_HEADS
ROPE_THETA = 10000.0
HGRN_HEADS = 8
HGRN_DK = 128

LANES = 128
SUBLANES = 8
BF16_ROWS = 16
MXU_ROWS = 256
VT_ROWS = HEAD_DIM + BF16_ROWS
VMEM_LIMIT = 56 * 1024 * 1024

HGRN_CHUNK = 128
HGRN_STEP = 1024
LOG2_E = float(np.log2(np.e))


def _params(*sem):
    return pltpu.CompilerParams(dimension_semantics=sem, vmem_limit_bytes=VMEM_LIMIT)


def _dot(a, b):
    return jnp.dot(a, b, preferred_element_type=F32)


def _dot_nt(a, b):
    return lax.dot_general(a, b, (((1,), (1,)), ((), ())), preferred_element_type=F32)


def _sigmoid(x):
    return 1.0 / (1.0 + jnp.exp(-x))


def _rms(x):
    return x * lax.rsqrt(jnp.mean(x * x, axis=-1, keepdims=True) + NORM_EPS)


def _row_subtiles(rows):
    size = MXU_ROWS if rows % MXU_ROWS == 0 else rows
    return [slice(r, r + size) for r in range(0, rows, size)]


def _adaln_kernel(c_ref, w_ref, b_ref, o_ref):
    c = c_ref[...]
    a = (c * _sigmoid(c)).astype(BF16)
    o_ref[...] = _dot(a, w_ref[...].astype(BF16)) + b_ref[...]


def _adaln(cc, ada_w, ada_b):
    depth, d, n = ada_w.shape
    rows = cc.shape[0]
    tn = 1536
    return pl.pallas_call(
        _adaln_kernel,
        grid=(depth, n // tn),
        in_specs=[
            pl.BlockSpec((rows, d), lambda i, j: (0, 0)),
            pl.BlockSpec((None, d, tn), lambda i, j: (i, 0, j)),
            pl.BlockSpec((None, 1, tn), lambda i, j: (i, 0, j)),
        ],
        out_specs=pl.BlockSpec((None, rows, tn), lambda i, j: (i, 0, j)),
        out_shape=jax.ShapeDtypeStruct((depth, rows, n), F32),
        compiler_params=_params("parallel", "parallel"),
        name="adaln",
    )(cc, ada_w, ada_b.reshape(depth, 1, n))


def _swap16_index(n):
    j = np.arange(n)
    return np.where((j % 32) < 16, j + 16, j - 16)


def _head_norm_rope(x, x_sw, w, w_sw, cos, sin):
    lane = lax.broadcasted_iota(jnp.int32, x.shape, 1)
    lo = lane < HEAD_DIM
    sq = x * x
    s_lo = jnp.sum(jnp.where(lo, sq, 0.0), axis=-1, keepdims=True)
    s_hi = jnp.sum(jnp.where(lo, 0.0, sq), axis=-1, keepdims=True)
    ms = jnp.where(lo, s_lo, s_hi) * (1.0 / HEAD_DIM)
    rinv = lax.rsqrt(ms + NORM_EPS)
    xn = x * rinv * w
    if cos is None:
        return xn
    return xn * cos + (x_sw * rinv * w_sw) * sin


def _qkv_kernel(*refs, rope, q_scale):
    if rope:
        (x_ref, nw_ref, sh_ref, sc_ref, w_ref, qn_ref, kn_ref, qns_ref, kns_ref, cos_ref, sin_ref,
         qt_ref, k_ref, vt_ref) = refs
        cos, sin = cos_ref[...], sin_ref[...]
        qns, kns = qns_ref[...], kns_ref[...]
    else:
        x_ref, nw_ref, sh_ref, sc_ref, w_ref, qn_ref, kn_ref, qt_ref, k_ref, vt_ref = refs
        cos = sin = qns = kns = None
    x = x_ref[...]
    hn = (_rms(x) * nw_ref[...]) * (1.0 + sc_ref[...]) + sh_ref[...]
    z = _dot(hn.astype(BF16), w_ref[...])
    tm = z.shape[0]
    qw = qt_ref.shape[0]
    kw = k_ref.shape[0] * HEAD_DIM
    sw = qw + 2 * kw

    def slab(base, s):
        return z[:, base + s * LANES:base + (s + 1) * LANES]

    for s in range(qw // LANES):
        qn = _head_norm_rope(slab(0, s), slab(sw, s) if rope else None,
                             qn_ref[...], qns, cos, sin) * q_scale
        qt_ref[s * LANES:(s + 1) * LANES, :] = qn.T.astype(BF16)
    for s in range(kw // LANES):
        kn = _head_norm_rope(slab(qw, s), slab(sw + qw, s) if rope else None,
                             kn_ref[...], kns, cos, sin)
        k_ref[2 * s] = kn[:, :HEAD_DIM].astype(BF16)
        k_ref[2 * s + 1] = kn[:, HEAD_DIM:].astype(BF16)
    row = lax.broadcasted_iota(jnp.int32, (VT_ROWS - HEAD_DIM, tm), 0)
    pad = jnp.where(row == 0, 1.0, 0.0).astype(BF16)
    for s in range(kw // LANES):
        vt = z[:, qw + kw + s * LANES:qw + kw + (s + 1) * LANES].T.astype(BF16)
        for half in range(2):
            vt_ref[2 * s + half, :HEAD_DIM, :] = vt[half * HEAD_DIM:(half + 1) * HEAD_DIM]
            vt_ref[2 * s + half, HEAD_DIM:, :] = pad


def _qkv(h, nw, sh, sc, w_qkv, qn, kn, rope_tabs, tm):
    b, t, d = h.shape
    n = w_qkv.shape[1]
    qw = ATTN_HEADS * HEAD_DIM
    kw = ATTN_KV_HEADS * HEAD_DIM
    rope = rope_tabs is not None
    row = lambda i, j: (i, 0, 0)
    in_specs = [
        pl.BlockSpec((None, tm, d), lambda i, j: (i, j, 0)),
        pl.BlockSpec((1, d), lambda i, j: (0, 0)),
        pl.BlockSpec((None, 1, d), row),
        pl.BlockSpec((None, 1, d), row),
        pl.BlockSpec((d, n), lambda i, j: (0, 0)),
        pl.BlockSpec((1, LANES), lambda i, j: (0, 0)),
        pl.BlockSpec((1, LANES), lambda i, j: (0, 0)),
    ]
    args = [h, nw, sh, sc, w_qkv, qn, kn]
    if rope:
        swap = _swap16_index(LANES)
        in_specs += [pl.BlockSpec((1, LANES), lambda i, j: (0, 0))] * 2
        in_specs += [pl.BlockSpec((tm, LANES), lambda i, j: (j, 0))] * 2
        args += [qn[:, swap], kn[:, swap]] + list(rope_tabs)
    return pl.pallas_call(
        functools.partial(_qkv_kernel, rope=rope, q_scale=HEAD_DIM ** -0.5 * LOG2_E),
        grid=(b, t // tm),
        in_specs=in_specs,
        out_specs=[
            pl.BlockSpec((None, qw, tm), lambda i, j: (i, 0, j)),
            pl.BlockSpec((None, ATTN_KV_HEADS, tm, HEAD_DIM), lambda i, j: (i, 0, j, 0)),
            pl.BlockSpec((None, ATTN_KV_HEADS, VT_ROWS, tm), lambda i, j: (i, 0, 0, j)),
        ],
        out_shape=[
            jax.ShapeDtypeStruct((b, qw, t), BF16),
            jax.ShapeDtypeStruct((b, ATTN_KV_HEADS, t, HEAD_DIM), BF16),
            jax.ShapeDtypeStruct((b, ATTN_KV_HEADS, VT_ROWS, t), BF16),
        ],
        compiler_params=_params("parallel", "parallel"),
        name="qkv_rope" if rope else "qkv_ctx",
    )(*args)


def _rope_tables(n_tokens):
    axis_dim = HEAD_DIM // 2
    pos = np.arange(n_tokens)
    inv_freq = ROPE_THETA ** (-(np.arange(axis_dim // 2, dtype=np.float32) * 2.0 / axis_dim))
    ang_r = (pos // GRID_W).astype(np.float32)[:, None] * inv_freq
    ang_c = (pos % GRID_W).astype(np.float32)[:, None] * inv_freq
    cos = np.concatenate([np.cos(ang_r)] * 2 + [np.cos(ang_c)] * 2, axis=1)
    sin = np.concatenate([-np.sin(ang_r), np.sin(ang_r), -np.sin(ang_c), np.sin(ang_c)], axis=1)
    reps = LANES // HEAD_DIM
    return (jnp.asarray(np.tile(cos, (1, reps)), F32), jnp.asarray(np.tile(sin, (1, reps)), F32))


def _attn_kernel(qt_ref, k_ref, vt_ref, o_ref, *bufs, tk):
    nk = k_ref.shape[0] // tk
    n_sub = len(bufs) // 4
    tq = qt_ref.shape[1] // n_sub

    class SubTile:
        def __init__(self, idx):
            self.idx = idx
            self.s_bufs, self.p_bufs = bufs[4 * idx:4 * idx + 2], bufs[4 * idx + 2:4 * idx + 4]
            qt = qt_ref[:, idx * tq:(idx + 1) * tq]
            self.qs = jnp.concatenate(
                [qt[g * HEAD_DIM:(g + 1) * HEAD_DIM] for g in range(ATTN_GROUP)], axis=1)
            cols = self.qs.shape[1]
            self.m = jnp.full((1, cols), -jnp.inf, F32)
            self.acc = jnp.zeros((VT_ROWS, cols), F32)
            self.alpha = [None, None]

        def tick(self, t):
            if 0 <= t < nk:
                self.s_bufs[t % 2][...] = _dot(k_ref[t * tk:(t + 1) * tk, :], self.qs)
            j = t - 1
            if 0 <= j < nk:
                s = self.s_bufs[j % 2][...]
                m_new = jnp.maximum(self.m, jnp.max(s, axis=0, keepdims=True))
                self.p_bufs[j % 2][...] = jnp.exp2(s - m_new).astype(BF16)
                self.alpha[j % 2] = jnp.exp2(self.m - m_new)
                self.m = m_new
            j = t - 2
            if 0 <= j < nk:
                self.acc = self.alpha[j % 2] * self.acc + _dot(
                    vt_ref[:, j * tk:(j + 1) * tk], self.p_bufs[j % 2][...])
            if j == nk - 1:
                o = self.acc / self.acc[HEAD_DIM:HEAD_DIM + 1]
                o = jnp.concatenate([o, jnp.zeros((LANES - VT_ROWS, o.shape[1]), F32)], axis=0).T
                o_ref[self.idx * tq:(self.idx + 1) * tq, :] = jnp.concatenate(
                    [o[g * tq:(g + 1) * tq, :HEAD_DIM] for g in range(ATTN_GROUP)],
                    axis=1).astype(BF16)

    subs = [SubTile(i) for i in range(n_sub)]
    for t in range(nk + 2 + n_sub - 1):
        for sub in subs:
            sub.tick(t - sub.idx)


def _attention(qt, k, vt, tq, tk, n_sub):
    b, qw, t = qt.shape
    s = k.shape[2]
    gw = ATTN_GROUP * HEAD_DIM
    sub_cols = ATTN_GROUP * tq // n_sub
    return pl.pallas_call(
        functools.partial(_attn_kernel, tk=tk),
        grid=(b, ATTN_KV_HEADS, t // tq),
        in_specs=[
            pl.BlockSpec((None, gw, tq), lambda i, k, j: (i, k, j)),
            pl.BlockSpec((None, None, s, HEAD_DIM), lambda i, k, j: (i, k, 0, 0)),
            pl.BlockSpec((None, None, VT_ROWS, s), lambda i, k, j: (i, k, 0, 0)),
        ],
        out_specs=pl.BlockSpec((None, tq, gw), lambda i, k, j: (i, j, k)),
        out_shape=jax.ShapeDtypeStruct((b, t, qw), BF16),
        scratch_shapes=([pltpu.VMEM((tk, sub_cols), F32)] * 2
                        + [pltpu.VMEM((tk, sub_cols), BF16)] * 2) * n_sub,
        compiler_params=_params("parallel", "parallel", "parallel"),
        name="attention",
    )(qt, k, vt)


def _post_kernel(*refs, mode, final, ffn_splits):
    if mode == "attn":
        (h_ref, y_ref, wo_ref, g1_ref, nw_ref, sh_ref, sc_ref, g2_ref, wi_ref, wout_ref) = refs[:10]
        rest = refs[10:]
    else:
        (h_ref, of_ref, ob_ref, gate_ref, on_ref, wo_ref, g1_ref, nw_ref, sh_ref, sc_ref, g2_ref,
         wi_ref, wout_ref) = refs[:13]
        rest = refs[13:]
    if final:
        fw_ref, out_ref = rest
    else:
        (out_ref,) = rest
    hidden = wout_ref.shape[0]
    if mode == "attn":
        y_in = y_ref[...]
    else:
        o = of_ref[...] + ob_ref[...]
        gate = gate_ref[...]
        parts = []
        for hh in range(o.shape[1] // HGRN_DK):
            sl = slice(hh * HGRN_DK, (hh + 1) * HGRN_DK)
            parts.append(_rms(o[:, sl]) * on_ref[:, sl] * _sigmoid(gate[:, sl]))
        y_in = jnp.concatenate(parts, axis=1).astype(BF16)
    h1 = h_ref[...] + g1_ref[...] * _dot(y_in, wo_ref[...])
    hn = ((_rms(h1) * nw_ref[...]) * (1.0 + sc_ref[...]) + sh_ref[...]).astype(BF16)
    acc = None
    off = 0
    for width in ffn_splits:
        a = _dot(hn, wi_ref[:, off:off + width])
        u = _dot(hn, wi_ref[:, hidden + off:hidden + off + width])
        gu = (a * _sigmoid(a) * u).astype(BF16)
        part = _dot(gu, wout_ref[off:off + width, :])
        acc = part if acc is None else acc + part
        off += width
    h2 = h1 + g2_ref[...] * acc
    if final:
        h2 = _rms(h2) * fw_ref[...]
    out_ref[...] = h2


def _post(h, mixer_in, wo, g1, nw, sh, sc, g2, wi, wout, final_w, mode, tm):
    b, t, d = h.shape
    hidden = wout.shape[0]
    tile = pl.BlockSpec((None, tm, d), lambda i, j: (i, j, 0))
    row = pl.BlockSpec((None, 1, d), lambda i, j: (i, 0, 0))
    const = lambda shape: pl.BlockSpec(shape, lambda i, j: (0,) * len(shape),
                                       pipeline_mode=pl.Buffered(1))
    if mode == "attn":
        mix_specs = [pl.BlockSpec((None, tm, mixer_in[0].shape[-1]), lambda i, j: (i, j, 0))]
    else:
        mix_specs = [tile, tile, tile, const((1, d))]
    in_specs = [tile] + mix_specs + [const(wo.shape), row, const((1, d)), row, row, row,
                                     const(wi.shape), const(wout.shape)]
    args = [h] + list(mixer_in) + [wo, g1, nw, sh, sc, g2, wi, wout]
    final = final_w is not None
    if final:
        in_specs.append(const((1, d)))
        args.append(final_w)
    splits = (1536, hidden - 1536) if hidden > 1536 else (hidden,)
    return pl.pallas_call(
        functools.partial(_post_kernel, mode=mode, final=final, ffn_splits=splits),
        grid=(b, t // tm),
        in_specs=in_specs,
        out_specs=tile,
        out_shape=jax.ShapeDtypeStruct((b, t, d), F32),
        compiler_params=_params("parallel", "parallel"),
        name="post_" + mode + ("_final" if final else ""),
    )(*args)


def _hgrn_in_kernel(x_ref, nw_ref, sh_ref, sc_ref, w_ref, lbl_ref,
                    q_ref, gate_ref, gf_ref, kf_ref, gb_ref, kb_ref, v_ref, *, layer):
    wd = q_ref.shape[-1]
    lg = lbl_ref[...]
    e = jnp.exp(lg - jnp.max(lg, axis=0, keepdims=True))
    p = e / jnp.sum(e, axis=0, keepdims=True)
    lb = jnp.sum(p[1:layer + 1], axis=0, keepdims=True) if layer > 0 else jnp.zeros_like(p[0:1])

    for rows in _row_subtiles(x_ref.shape[0]):
        x = x_ref[rows, :]
        hn = ((_rms(x) * nw_ref[...]) * (1.0 + sc_ref[...]) + sh_ref[...]).astype(BF16)

        def proj(i):
            return _dot(hn, w_ref[:, i * wd:(i + 1) * wd])

        q_ref[rows, :] = proj(0).astype(BF16)
        gate_ref[rows, :] = proj(1)
        for i, (g_ref, k_ref) in ((2, (gf_ref, kf_ref)), (3, (gb_ref, kb_ref))):
            f = lb + (1.0 - lb) * _sigmoid(proj(i))
            g_ref[rows, :] = jnp.log(f)
            k_ref[rows, :] = (1.0 - f).astype(BF16)
        v_ref[rows, :] = proj(4).astype(BF16)


def _hgrn_in(h, nw, sh, sc, w_in, lb_logits, layer, tm):
    b, t, d = h.shape
    wd = w_in.shape[1] // 5
    tile = lambda: pl.BlockSpec((None, tm, wd), lambda i, j: (i, j, 0))
    row = pl.BlockSpec((None, 1, d), lambda i, j: (i, 0, 0))
    depth = lb_logits.shape[0]
    dts = (BF16, F32, F32, BF16, F32, BF16, BF16)
    return pl.pallas_call(
        functools.partial(_hgrn_in_kernel, layer=layer),
        grid=(b, t // tm),
        in_specs=[
            pl.BlockSpec((None, tm, d), lambda i, j: (i, j, 0)),
            pl.BlockSpec((1, d), lambda i, j: (0, 0)),
            row, row,
            pl.BlockSpec(w_in.shape, lambda i, j: (0, 0), pipeline_mode=pl.Buffered(1)),
            pl.BlockSpec((depth, wd), lambda i, j: (0, 0)),
        ],
        out_specs=[tile() for _ in dts],
        out_shape=[jax.ShapeDtypeStruct((b, t, wd), dt) for dt in dts],
        compiler_params=_params("parallel", "parallel"),
        name="hgrn_in",
    )(h, nw, sh, sc, w_in, lb_logits)


def _chunk_tables(c, rev):
    levels = int(np.log2(c))
    idx = np.arange(c)
    t, j = idx[:, None], idx[None, :]
    mats = [(j <= t)]
    masks = [np.eye(c, dtype=bool)]
    for lv in range(1, levels + 1):
        blk = 2 ** lv
        mid = (idx // blk) * blk + blk // 2
        upper = (idx % blk) >= blk // 2
        m_up = upper[:, None] & (j >= mid[:, None]) & (j <= t)
        m_lo = (~upper)[:, None] & (j > t) & (j <= mid[:, None] - 1)
        if blk <= SUBLANES:
            mats.append(m_up | m_lo)
        same = (idx[:, None] // blk) == (idx[None, :] // blk)
        masks.append(same & upper[:, None] & (~upper)[None, :])
    mats = np.stack(mats).astype(np.float32)
    masks = np.stack(masks).astype(np.float32)
    if rev:
        mats = mats[:, ::-1, ::-1]
        masks = masks[:, ::-1, ::-1]
    return (jnp.asarray(mats.reshape(-1, c), BF16), jnp.asarray(masks, F32))


def _split_bf16(g):
    hi = g.astype(BF16)
    return hi, (g - hi.astype(F32)).astype(BF16)


def _level_exponents(cum, blk, rev):
    half = blk // 2
    parts = []
    for b0 in range(0, cum.shape[0], blk):
        lo, up = cum[b0:b0 + half], cum[b0 + half:b0 + blk]
        if rev:
            ref = cum[b0 + half:b0 + half + 1]
            parts += [lo - ref, ref - up]
        else:
            ref = cum[b0 + half - 1:b0 + half]
            parts += [ref - lo, up - ref]
    return jnp.concatenate(parts, axis=0)


def _hgrn_tables(g, mst):
    g_hi, g_lo = _split_bf16(g * LOG2_E)
    return _dot(mst, g_hi) + _dot(mst, g_lo)


def _hgrn_intra(q, k, v, e, masks, rev):
    c = q.shape[0]
    cum = e[0:c]
    end = cum[0:1] if rev else cum[c - 1:c]
    qf, kf = q.astype(F32), k.astype(F32)
    a = masks[0] * _dot_nt(q, k)
    for lv in range(1, masks.shape[0]):
        blk = 2 ** lv
        ex = e[lv * c:(lv + 1) * c] if blk <= SUBLANES else _level_exponents(cum, blk, rev)
        d = jnp.exp2(ex)
        a = a + masks[lv] * _dot((qf * d).astype(BF16), (kf * d).T.astype(BF16))
    o_intra = _dot(a.astype(BF16), v)
    q_in = (qf * jnp.exp2(cum)).astype(BF16)
    kr = (kf * jnp.exp2(end - cum)).astype(BF16)
    st_add = _dot(v.astype(F32).T.astype(BF16), kr)
    return o_intra, q_in, jnp.exp2(end), st_add


def _hgrn_state_kernel(kf_ref, gf_ref, kb_ref, gb_ref, v_ref, tri_ref, sf_ref, sb_ref):
    v = v_ref[...]
    vt = v.astype(F32).T.astype(BF16)
    n = v.shape[0]
    tri = tri_ref[...]
    for d, (k_ref, g_ref, s_ref) in enumerate(((kf_ref, gf_ref, sf_ref), (kb_ref, gb_ref, sb_ref))):
        g_hi, g_lo = _split_bf16(g_ref[...])
        cum = _dot(tri[d], g_hi) + _dot(tri[d], g_lo)
        end = cum[n - 1:n] if d == 0 else cum[0:1]
        kr = (k_ref[...].astype(F32) * jnp.exp(end - cum)).astype(BF16)
        s_ref[...] = _dot(vt, kr)


def _hgrn_state(kf, gf, kb, gb, v):
    b, n, w = v.shape
    heads = w // HGRN_DK
    idx = np.arange(n)
    tri = jnp.asarray(np.stack([idx[None, :] <= idx[:, None], idx[None, :] >= idx[:, None]]), BF16)
    tile = pl.BlockSpec((None, n, HGRN_DK), lambda i, hh: (i, 0, hh))
    st = pl.BlockSpec((None, None, HGRN_DK, HGRN_DK), lambda i, hh: (i, hh, 0, 0))
    shape = jax.ShapeDtypeStruct((b, heads, HGRN_DK, HGRN_DK), F32)
    return pl.pallas_call(
        _hgrn_state_kernel,
        grid=(b, heads),
        in_specs=[tile] * 5 + [pl.BlockSpec((2, n, n), lambda i, hh: (0, 0, 0))],
        out_specs=[st, st],
        out_shape=[shape, shape],
        compiler_params=_params("parallel", "parallel"),
        name="hgrn_state",
    )(kf, gf, kb, gb, v, tri)


def _hgrn_scan_kernel(qf_ref, vf_ref, kf_ref, gf_ref, qb_ref, vb_ref, kb_ref, gb_ref,
                      s0f_ref, s0b_ref, mf_ref, mb_ref, mkf_ref, mkb_ref,
                      of_ref, ob_ref, stf_ref, stb_ref, *, chunk):
    @pl.when(pl.program_id(2) == 0)
    def _():
        stf_ref[...] = s0f_ref[...]
        stb_ref[...] = s0b_ref[...]

    n_chunks = qf_ref.shape[0] // chunk
    mf, mb = mf_ref[...], mb_ref[...]
    bodies = []
    for ci in range(n_chunks):
        bodies.append((slice(ci * chunk, (ci + 1) * chunk), False))
        rc = n_chunks - 1 - ci
        bodies.append((slice(rc * chunk, (rc + 1) * chunk), True))
    refs = {False: (qf_ref, kf_ref, gf_ref, vf_ref, mf, mkf_ref, of_ref),
            True: (qb_ref, kb_ref, gb_ref, vb_ref, mb, mkb_ref, ob_ref)}
    tables = [_hgrn_tables(refs[rev][2][sl, :], refs[rev][4]) for sl, rev in bodies]
    parts = []
    for (sl, rev), e in zip(bodies, tables):
        q_ref, k_ref, _, v_ref, _, masks, _ = refs[rev]
        parts.append(_hgrn_intra(q_ref[sl, :], k_ref[sl, :], v_ref[sl, :], e, masks, rev))
    st = {False: stf_ref[...], True: stb_ref[...]}
    for (sl, rev), (o_intra, q_in, dec_end, st_add) in zip(bodies, parts):
        refs[rev][6][sl, :] = o_intra + _dot_nt(q_in, st[rev].astype(BF16))
        st[rev] = st[rev] * dec_end + st_add
    stf_ref[...] = st[False]
    stb_ref[...] = st[True]


def _hgrn_scan(q, v, kf, gf, kb, gb, s0f, s0b, step, chunk):
    b, n, w = q.shape
    heads = w // HGRN_DK
    nt = n // step
    mf, mkf = _chunk_tables(chunk, False)
    mb, mkb = _chunk_tables(chunk, True)
    fwd = pl.BlockSpec((None, step, HGRN_DK), lambda i, hh, t: (i, t, hh))
    bwd = pl.BlockSpec((None, step, HGRN_DK), lambda i, hh, t: (i, nt - 1 - t, hh))
    st = pl.BlockSpec((None, None, HGRN_DK, HGRN_DK), lambda i, hh, t: (i, hh, 0, 0))
    c2 = pl.BlockSpec(mf.shape, lambda i, hh, t: (0, 0))
    c3 = pl.BlockSpec(mkf.shape, lambda i, hh, t: (0, 0, 0))
    out = jax.ShapeDtypeStruct((b, n, w), F32)
    return pl.pallas_call(
        functools.partial(_hgrn_scan_kernel, chunk=chunk),
        grid=(b, heads, nt),
        in_specs=[fwd] * 4 + [bwd] * 4 + [st, st, c2, c2, c3, c3],
        out_specs=[fwd, bwd],
        out_shape=[out, out],
        scratch_shapes=[pltpu.VMEM((HGRN_DK, HGRN_DK), F32)] * 2,
        compiler_params=_params("parallel", "parallel", "arbitrary"),
        name="hgrn_scan",
    )(q, v, kf, gf, q, v, kb, gb, s0f, s0b, mf, mb, mkf, mkb)


def kernel(x, c, ctx, c_ctx, ada_w, ada_b, norm_mix_w, norm_ffn_w, attn_w_qkv, attn_q_norm,
           attn_k_norm, attn_w_o, hgrn_w_in, hgrn_lb_logits, hgrn_out_norm, hgrn_w_o,
           ffn_w_in, ffn_w_out, final_norm_w):
    b, n, d = x.shape
    m = ctx.shape[1]
    depth = ada_w.shape[0]
    tm = min(512, n)

    rows = -(-(b + 1) // 8) * 8
    cc = jnp.zeros((rows, d), F32).at[:b].set(c).at[b].set(c_ctx)
    mod = _adaln(cc, ada_w, ada_b)

    def mods(i, lat):
        parts = []
        for s in range(6):
            blk = mod[i, :, s * d:(s + 1) * d]
            if lat:
                parts.append(blk[:b, None, :])
            else:
                parts.append(jnp.broadcast_to(blk[b][None, None, :], (b, 1, d)))
        return parts

    h, hc = x, ctx
    for i in range(depth):
        last = i == depth - 1
        j = i // 2
        sh1, sc1, gt1, sh2, sc2, gt2 = mods(i, True)
        csh1, csc1, cgt1, csh2, csc2, cgt2 = mods(i, False)
        nmw = norm_mix_w[i][None, :]
        nfw = norm_ffn_w[i][None, :]
        wi = ffn_w_in[i].astype(BF16)
        wout = ffn_w_out[i].astype(BF16)
        fin = final_norm_w[None, :] if last else None
        if i % 2 == 0:
            w_qkv = attn_w_qkv[j].astype(BF16)
            wo = attn_w_o[j].astype(BF16)
            qn = jnp.tile(attn_q_norm[j], LANES // HEAD_DIM)[None, :]
            kn = jnp.tile(attn_k_norm[j], LANES // HEAD_DIM)[None, :]
            roped = ATTN_HEADS * HEAD_DIM + ATTN_KV_HEADS * HEAD_DIM
            w_rope = jnp.concatenate([w_qkv, w_qkv[:, _swap16_index(roped)]], axis=1)
            q_l, k_l, v_l = _qkv(h, nmw, sh1, sc1, w_rope, qn, kn, _rope_tables(n), tm)
            q_c, k_c, v_c = _qkv(hc, nmw, csh1, csc1, w_qkv, qn, kn, None, m)
            k_all = jnp.concatenate([k_c, k_l], axis=2)
            v_all = jnp.concatenate([v_c, v_l], axis=3)
            o_l = _attention(q_l, k_all, v_all, 256, 768 if (m + n) % 768 == 0 else m, 1)
            h = _post(h, [o_l], wo, gt1, nfw, sh2, sc2, gt2, wi, wout, fin, "attn", tm)
            if not last:
                o_c = _attention(q_c, k_c, v_c, m, m, 1)
                hc = _post(hc, [o_c], wo, cgt1, nfw, csh2, csc2, cgt2, wi, wout, None, "attn", m)
        else:
            w_in = hgrn_w_in[j].astype(BF16)
            wo = hgrn_w_o[j].astype(BF16)
            on = hgrn_out_norm[j][None, :]
            q, gate, gf, kf, gb, kb, v = _hgrn_in(h, nmw, sh1, sc1, w_in, hgrn_lb_logits, i, tm)
            cq, cgate, cgf, ckf, cgb, ckb, cv = _hgrn_in(hc, nmw, csh1, csc1, w_in,
                                                         hgrn_lb_logits, i, m)
            if last:
                s0f, s0b = _hgrn_state(ckf, cgf, ckb, cgb, cv)
            else:
                zero = jnp.zeros((b, HGRN_HEADS, HGRN_DK, HGRN_DK), F32)
                oc_f, oc_b = _hgrn_scan(cq, cv, ckf, cgf, ckb, cgb, zero, zero, m, HGRN_CHUNK)
                s0f, s0b = _hgrn_state(ckf, cgf, ckb, cgb, cv)
                hc = _post(hc, [oc_f, oc_b, cgate, on], wo, cgt1, nfw, csh2, csc2, cgt2, wi, wout,
                           None, "hgrn", m)
            o_f, o_b = _hgrn_scan(q, v, kf, gf, kb, gb, s0f, s0b, min(HGRN_STEP, n), HGRN_CHUNK)
            h = _post(h, [o_f, o_b, gate, on], wo, gt1, nfw, sh2, sc2, gt2, wi, wout, fin, "hgrn", tm)
    return h
```

```python
import functools

import numpy as np
import jax
import jax.numpy as jnp
from jax import lax
from jax.experimental import pallas as pl
from jax.experimental.pallas import tpu as pltpu

F32 = jnp.float32
BF16 = jnp.bfloat16

NORM_EPS = 1e-6
GRID_W = 64
ATTN_HEADS = 16
ATTN_KV_HEADS = 4
HEAD_DIM = 64
ATTN_GROUP = ATTN_HEADS // ATTN_KV_HEADS
ROPE_THETA = 10000.0
HGRN_HEADS = 8
HGRN_DK = 128

LANES = 128
SUBLANES = 8
BF16_ROWS = 16
MXU_ROWS = 256
VT_ROWS = LANES
VMEM_LIMIT = 56 * 1024 * 1024

HGRN_CHUNK = 128
HGRN_STEP = 1024
LOG2_E = float(np.log2(np.e))


def _params(*sem):
    return pltpu.CompilerParams(dimension_semantics=sem, vmem_limit_bytes=VMEM_LIMIT)


def _dot(a, b):
    return jnp.dot(a, b, preferred_element_type=F32)


def _dot_nt(a, b):
    return lax.dot_general(a, b, (((1,), (1,)), ((), ())), preferred_element_type=F32)


def _sigmoid(x):
    return 1.0 / (1.0 + jnp.exp(-x))


def _rms(x):
    return x * lax.rsqrt(jnp.mean(x * x, axis=-1, keepdims=True) + NORM_EPS)


def _row_subtiles(rows):
    size = MXU_ROWS if rows % MXU_ROWS == 0 else rows
    return [slice(r, r + size) for r in range(0, rows, size)]


def _adaln_kernel(c_ref, w_ref, b_ref, o_ref):
    c = c_ref[...]
    a = (c * _sigmoid(c)).astype(BF16)
    o_ref[...] = _dot(a, w_ref[...].astype(BF16)) + b_ref[...]


def _adaln(cc, ada_w, ada_b):
    depth, d, n = ada_w.shape
    rows = cc.shape[0]
    tn = 1536
    return pl.pallas_call(
        _adaln_kernel,
        grid=(depth, n // tn),
        in_specs=[
            pl.BlockSpec((rows, d), lambda i, j: (0, 0)),
            pl.BlockSpec((None, d, tn), lambda i, j: (i, 0, j)),
            pl.BlockSpec((None, 1, tn), lambda i, j: (i, 0, j)),
        ],
        out_specs=pl.BlockSpec((None, rows, tn), lambda i, j: (i, 0, j)),
        out_shape=jax.ShapeDtypeStruct((depth, rows, n), F32),
        compiler_params=_params("parallel", "parallel"),
        name="adaln",
    )(cc, ada_w, ada_b.reshape(depth, 1, n))


def _swap16_index(n):
    j = np.arange(n)
    return np.where((j % 32) < 16, j + 16, j - 16)


def _head_norm_rope(x, x_sw, w, w_sw, cos, sin):
    lane = lax.broadcasted_iota(jnp.int32, x.shape, 1)
    lo = lane < HEAD_DIM
    sq = x * x
    s_lo = jnp.sum(jnp.where(lo, sq, 0.0), axis=-1, keepdims=True)
    s_hi = jnp.sum(jnp.where(lo, 0.0, sq), axis=-1, keepdims=True)
    ms = jnp.where(lo, s_lo, s_hi) * (1.0 / HEAD_DIM)
    rinv = lax.rsqrt(ms + NORM_EPS)
    xn = x * rinv * w
    if cos is None:
        return xn
    return xn * cos + (x_sw * rinv * w_sw) * sin


def _qkv_kernel(*refs, rope, q_scale):
    if rope:
        (x_ref, nw_ref, sh_ref, sc_ref, w_ref, qn_ref, kn_ref, qns_ref, kns_ref, cos_ref, sin_ref,
         qt_ref, k_ref, vt_ref) = refs
        cos, sin = cos_ref[...], sin_ref[...]
        qns, kns = qns_ref[...], kns_ref[...]
    else:
        x_ref, nw_ref, sh_ref, sc_ref, w_ref, qn_ref, kn_ref, qt_ref, k_ref, vt_ref = refs
        cos = sin = qns = kns = None
    x = x_ref[...]
    hn = (_rms(x) * nw_ref[...]) * (1.0 + sc_ref[...]) + sh_ref[...]
    z = _dot(hn.astype(BF16), w_ref[...])
    tm = z.shape[0]
    qw = qt_ref.shape[0]
    kw = k_ref.shape[0] * HEAD_DIM
    sw = qw + 2 * kw

    def slab(base, s):
        return z[:, base + s * LANES:base + (s + 1) * LANES]

    for s in range(qw // LANES):
        qn = _head_norm_rope(slab(0, s), slab(sw, s) if rope else None,
                             qn_ref[...], qns, cos, sin) * q_scale
        qt_ref[s * LANES:(s + 1) * LANES, :] = qn.T.astype(BF16)
    for s in range(kw // LANES):
        kn = _head_norm_rope(slab(qw, s), slab(sw + qw, s) if rope else None,
                             kn_ref[...], kns, cos, sin)
        k_ref[2 * s] = kn[:, :HEAD_DIM].astype(BF16)
        k_ref[2 * s + 1] = kn[:, HEAD_DIM:].astype(BF16)
    row = lax.broadcasted_iota(jnp.int32, (VT_ROWS - HEAD_DIM, tm), 0)
    pad = jnp.where(row == 0, 1.0, 0.0).astype(BF16)
    for s in range(kw // LANES):
        vt = z[:, qw + kw + s * LANES:qw + kw + (s + 1) * LANES].T.astype(BF16)
        for half in range(2):
            vt_ref[2 * s + half, :HEAD_DIM, :] = vt[half * HEAD_DIM:(half + 1) * HEAD_DIM]
            vt_ref[2 * s + half, HEAD_DIM:, :] = pad


def _qkv(h, nw, sh, sc, w_qkv, qn, kn, rope_tabs, tm):
    b, t, d = h.shape
    n = w_qkv.shape[1]
    qw = ATTN_HEADS * HEAD_DIM
    kw = ATTN_KV_HEADS * HEAD_DIM
    rope = rope_tabs is not None
    row = lambda i, j: (i, 0, 0)
    in_specs = [
        pl.BlockSpec((None, tm, d), lambda i, j: (i, j, 0)),
        pl.BlockSpec((1, d), lambda i, j: (0, 0)),
        pl.BlockSpec((None, 1, d), row),
        pl.BlockSpec((None, 1, d), row),
        pl.BlockSpec((d, n), lambda i, j: (0, 0)),
        pl.BlockSpec((1, LANES), lambda i, j: (0, 0)),
        pl.BlockSpec((1, LANES), lambda i, j: (0, 0)),
    ]
    args = [h, nw, sh, sc, w_qkv, qn, kn]
    if rope:
        swap = _swap16_index(LANES)
        in_specs += [pl.BlockSpec((1, LANES), lambda i, j: (0, 0))] * 2
        in_specs += [pl.BlockSpec((tm, LANES), lambda i, j: (j, 0))] * 2
        args += [qn[:, swap], kn[:, swap]] + list(rope_tabs)
    return pl.pallas_call(
        functools.partial(_qkv_kernel, rope=rope, q_scale=HEAD_DIM ** -0.5 * LOG2_E),
        grid=(b, t // tm),
        in_specs=in_specs,
        out_specs=[
            pl.BlockSpec((None, qw, tm), lambda i, j: (i, 0, j)),
            pl.BlockSpec((None, ATTN_KV_HEADS, tm, HEAD_DIM), lambda i, j: (i, 0, j, 0)),
            pl.BlockSpec((None, ATTN_KV_HEADS, VT_ROWS, tm), lambda i, j: (i, 0, 0, j)),
        ],
        out_shape=[
            jax.ShapeDtypeStruct((b, qw, t), BF16),
            jax.ShapeDtypeStruct((b, ATTN_KV_HEADS, t, HEAD_DIM), BF16),
            jax.ShapeDtypeStruct((b, ATTN_KV_HEADS, VT_ROWS, t), BF16),
        ],
        compiler_params=_params("parallel", "parallel"),
        name="qkv_rope" if rope else "qkv_ctx",
    )(*args)


def _rope_tables(n_tokens):
    axis_dim = HEAD_DIM // 2
    pos = np.arange(n_tokens)
    inv_freq = ROPE_THETA ** (-(np.arange(axis_dim // 2, dtype=np.float32) * 2.0 / axis_dim))
    ang_r = (pos // GRID_W).astype(np.float32)[:, None] * inv_freq
    ang_c = (pos % GRID_W).astype(np.float32)[:, None] * inv_freq
    cos = np.concatenate([np.cos(ang_r)] * 2 + [np.cos(ang_c)] * 2, axis=1)
    sin = np.concatenate([-np.sin(ang_r), np.sin(ang_r), -np.sin(ang_c), np.sin(ang_c)], axis=1)
    reps = LANES // HEAD_DIM
    return (jnp.asarray(np.tile(cos, (1, reps)), F32), jnp.asarray(np.tile(sin, (1, reps)), F32))


def _key_span(refs, lo, hi, axis):
    out, base = [], 0
    for r in refs:
        n = r.shape[axis]
        a, b = max(lo - base, 0), min(hi - base, n)
        if a < b:
            out.append(r[a:b, :] if axis == 0 else r[:, a:b])
        base += n
    return out[0] if len(out) == 1 else jnp.concatenate(out, axis=axis)


def _attn_kernel(qt_ref, *refs, tk, n_parts):
    k_refs, vt_refs = refs[:n_parts], refs[n_parts:2 * n_parts]
    o_ref, bufs = refs[2 * n_parts], refs[2 * n_parts + 1:]
    nk = sum(r.shape[0] for r in k_refs) // tk
    tq = qt_ref.shape[1]
    s_bufs, p_bufs = bufs[0:2], bufs[2:4]
    qt = qt_ref[...]
    qs = jnp.concatenate([qt[g * HEAD_DIM:(g + 1) * HEAD_DIM] for g in range(ATTN_GROUP)], axis=1)
    cols = qs.shape[1]
    m = jnp.full((1, cols), -jnp.inf, F32)
    acc = jnp.zeros((VT_ROWS, cols), F32)
    alpha = [None, None]
    for t in range(nk + 2):
        if t < nk:
            s_bufs[t % 2][...] = _dot(_key_span(k_refs, t * tk, (t + 1) * tk, 0), qs)
        j = t - 1
        if 0 <= j < nk:
            s = s_bufs[j % 2][...]
            m_new = jnp.maximum(m, jnp.max(s, axis=0, keepdims=True))
            p_bufs[j % 2][...] = jnp.exp2(s - m_new).astype(BF16)
            alpha[j % 2] = jnp.exp2(m - m_new)
            m = m_new
        j = t - 2
        if 0 <= j < nk:
            acc = alpha[j % 2] * acc + _dot(
                _key_span(vt_refs, j * tk, (j + 1) * tk, 1), p_bufs[j % 2][...])
    o = (acc / acc[HEAD_DIM:HEAD_DIM + 1]).T
    o_ref[...] = jnp.concatenate(
        [o[g * tq:(g + 1) * tq, :HEAD_DIM] for g in range(ATTN_GROUP)], axis=1).astype(BF16)


def _attention(qt, ks, vts, tq, tk):
    b, qw, t = qt.shape
    gw = ATTN_GROUP * HEAD_DIM
    cols = ATTN_GROUP * tq
    k_specs = [pl.BlockSpec((None, None, k.shape[2], HEAD_DIM), lambda i, h, j: (i, h, 0, 0))
               for k in ks]
    vt_specs = [pl.BlockSpec((None, None, VT_ROWS, vt.shape[3]), lambda i, h, j: (i, h, 0, 0))
                for vt in vts]
    return pl.pallas_call(
        functools.partial(_attn_kernel, tk=tk, n_parts=len(ks)),
        grid=(b, ATTN_KV_HEADS, t // tq),
        in_specs=[pl.BlockSpec((None, gw, tq), lambda i, h, j: (i, h, j))] + k_specs + vt_specs,
        out_specs=pl.BlockSpec((None, tq, gw), lambda i, h, j: (i, j, h)),
        out_shape=jax.ShapeDtypeStruct((b, t, qw), BF16),
        scratch_shapes=[pltpu.VMEM((tk, cols), F32)] * 2 + [pltpu.VMEM((tk, cols), BF16)] * 2,
        compiler_params=_params("parallel", "parallel", "parallel"),
        name="attention",
    )(qt, *ks, *vts)


def _post_kernel(*refs, mode, final, ffn_splits):
    if mode == "attn":
        (h_ref, y_ref, wo_ref, g1_ref, nw_ref, sh_ref, sc_ref, g2_ref, wi_ref, wout_ref) = refs[:10]
        rest = refs[10:]
    else:
        (h_ref, of_ref, ob_ref, gate_ref, on_ref, wo_ref, g1_ref, nw_ref, sh_ref, sc_ref, g2_ref,
         wi_ref, wout_ref) = refs[:13]
        rest = refs[13:]
    if final:
        fw_ref, out_ref = rest
    else:
        (out_ref,) = rest
    hidden = wout_ref.shape[0]
    if mode == "attn":
        y_in = y_ref[...]
    else:
        o = of_ref[...] + ob_ref[...]
        gate = gate_ref[...]
        parts = []
        for hh in range(o.shape[1] // HGRN_DK):
            sl = slice(hh * HGRN_DK, (hh + 1) * HGRN_DK)
            parts.append(_rms(o[:, sl]) * on_ref[:, sl] * _sigmoid(gate[:, sl]))
        y_in = jnp.concatenate(parts, axis=1).astype(BF16)
    h1 = h_ref[...] + g1_ref[...] * _dot(y_in, wo_ref[...])
    hn = ((_rms(h1) * nw_ref[...]) * (1.0 + sc_ref[...]) + sh_ref[...]).astype(BF16)
    acc = None
    off = 0
    for width in ffn_splits:
        a = _dot(hn, wi_ref[:, off:off + width])
        u = _dot(hn, wi_ref[:, hidden + off:hidden + off + width])
        gu = (a * _sigmoid(a) * u).astype(BF16)
        part = _dot(gu, wout_ref[off:off + width, :])
        acc = part if acc is None else acc + part
        off += width
    h2 = h1 + g2_ref[...] * acc
    if final:
        h2 = _rms(h2) * fw_ref[...]
    out_ref[...] = h2


def _post(h, mixer_in, wo, g1, nw, sh, sc, g2, wi, wout, final_w, mode, tm):
    b, t, d = h.shape
    hidden = wout.shape[0]
    tile = pl.BlockSpec((None, tm, d), lambda i, j: (i, j, 0))
    row = pl.BlockSpec((None, 1, d), lambda i, j: (i, 0, 0))
    const = lambda shape: pl.BlockSpec(shape, lambda i, j: (0,) * len(shape),
                                       pipeline_mode=pl.Buffered(1))
    if mode == "attn":
        mix_specs = [pl.BlockSpec((None, tm, mixer_in[0].shape[-1]), lambda i, j: (i, j, 0))]
    else:
        mix_specs = [tile, tile, tile, const((1, d))]
    in_specs = [tile] + mix_specs + [const(wo.shape), row, const((1, d)), row, row, row,
                                     const(wi.shape), const(wout.shape)]
    args = [h] + list(mixer_in) + [wo, g1, nw, sh, sc, g2, wi, wout]
    final = final_w is not None
    if final:
        in_specs.append(const((1, d)))
        args.append(final_w)
    splits = (1536, hidden - 1536) if hidden > 1536 else (hidden,)
    return pl.pallas_call(
        functools.partial(_post_kernel, mode=mode, final=final, ffn_splits=splits),
        grid=(b, t // tm),
        in_specs=in_specs,
        out_specs=tile,
        out_shape=jax.ShapeDtypeStruct((b, t, d), F32),
        compiler_params=_params("parallel", "parallel"),
        name="post_" + mode + ("_final" if final else ""),
    )(*args)


def _hgrn_in_kernel(x_ref, nw_ref, sh_ref, sc_ref, w_ref, lbl_ref,
                    q_ref, gate_ref, gf_ref, kf_ref, gb_ref, kb_ref, v_ref, *, layer):
    wd = q_ref.shape[-1]
    lg = lbl_ref[...]
    e = jnp.exp(lg - jnp.max(lg, axis=0, keepdims=True))
    p = e / jnp.sum(e, axis=0, keepdims=True)
    lb = jnp.sum(p[1:layer + 1], axis=0, keepdims=True) if layer > 0 else jnp.zeros_like(p[0:1])

    for rows in _row_subtiles(x_ref.shape[0]):
        x = x_ref[rows, :]
        hn = ((_rms(x) * nw_ref[...]) * (1.0 + sc_ref[...]) + sh_ref[...]).astype(BF16)

        def proj(i):
            return _dot(hn, w_ref[:, i * wd:(i + 1) * wd])

        q_ref[rows, :] = proj(0).astype(BF16)
        gate_ref[rows, :] = proj(1)
        for i, (g_ref, k_ref) in ((2, (gf_ref, kf_ref)), (3, (gb_ref, kb_ref))):
            f = lb + (1.0 - lb) * _sigmoid(proj(i))
            g_ref[rows, :] = jnp.log(f)
            k_ref[rows, :] = (1.0 - f).astype(BF16)
        v_ref[rows, :] = proj(4).astype(BF16)


def _hgrn_in(h, nw, sh, sc, w_in, lb_logits, layer, tm):
    b, t, d = h.shape
    wd = w_in.shape[1] // 5
    tile = lambda: pl.BlockSpec((None, tm, wd), lambda i, j: (i, j, 0))
    row = pl.BlockSpec((None, 1, d), lambda i, j: (i, 0, 0))
    depth = lb_logits.shape[0]
    dts = (BF16, F32, F32, BF16, F32, BF16, BF16)
    return pl.pallas_call(
        functools.partial(_hgrn_in_kernel, layer=layer),
        grid=(b, t // tm),
        in_specs=[
            pl.BlockSpec((None, tm, d), lambda i, j: (i, j, 0)),
            pl.BlockSpec((1, d), lambda i, j: (0, 0)),
            row, row,
            pl.BlockSpec(w_in.shape, lambda i, j: (0, 0), pipeline_mode=pl.Buffered(1)),
            pl.BlockSpec((depth, wd), lambda i, j: (0, 0)),
        ],
        out_specs=[tile() for _ in dts],
        out_shape=[jax.ShapeDtypeStruct((b, t, wd), dt) for dt in dts],
        compiler_params=_params("parallel", "parallel"),
        name="hgrn_in",
    )(h, nw, sh, sc, w_in, lb_logits)


def _chunk_tables(c, rev):
    levels = int(np.log2(c))
    idx = np.arange(c)
    t, j = idx[:, None], idx[None, :]
    mats = [(j <= t)]
    masks = [np.eye(c, dtype=bool)]
    for lv in range(1, levels + 1):
        blk = 2 ** lv
        mid = (idx // blk) * blk + blk // 2
        upper = (idx % blk) >= blk // 2
        m_up = upper[:, None] & (j >= mid[:, None]) & (j <= t)
        m_lo = (~upper)[:, None] & (j > t) & (j <= mid[:, None] - 1)
        if blk <= SUBLANES:
            mats.append(m_up | m_lo)
        same = (idx[:, None] // blk) == (idx[None, :] // blk)
        masks.append(same & upper[:, None] & (~upper)[None, :])
    mats = np.stack(mats).astype(np.float32)
    masks = np.stack(masks).astype(np.float32)
    if rev:
        mats = mats[:, ::-1, ::-1]
        masks = masks[:, ::-1, ::-1]
    return (jnp.asarray(mats.reshape(-1, c), BF16), jnp.asarray(masks, F32))


def _split_bf16(g):
    hi = g.astype(BF16)
    return hi, (g - hi.astype(F32)).astype(BF16)


def _level_exponents(cum, blk, rev):
    half = blk // 2
    parts = []
    for b0 in range(0, cum.shape[0], blk):
        lo, up = cum[b0:b0 + half], cum[b0 + half:b0 + blk]
        if rev:
            ref = cum[b0 + half:b0 + half + 1]
            parts += [lo - ref, ref - up]
        else:
            ref = cum[b0 + half - 1:b0 + half]
            parts += [ref - lo, up - ref]
    return jnp.concatenate(parts, axis=0)


def _hgrn_tables(g, mst):
    g_hi, g_lo = _split_bf16(g * LOG2_E)
    return _dot(mst, g_hi) + _dot(mst, g_lo)


def _hgrn_intra(q, k, v, e, masks, rev):
    c = q.shape[0]
    cum = e[0:c]
    end = cum[0:1] if rev else cum[c - 1:c]
    qf, kf = q.astype(F32), k.astype(F32)
    a = masks[0] * _dot_nt(q, k)
    for lv in range(1, masks.shape[0]):
        blk = 2 ** lv
        ex = e[lv * c:(lv + 1) * c] if blk <= SUBLANES else _level_exponents(cum, blk, rev)
        d = jnp.exp2(ex)
        a = a + masks[lv] * _dot((qf * d).astype(BF16), (kf * d).T.astype(BF16))
    o_intra = _dot(a.astype(BF16), v)
    q_in = (qf * jnp.exp2(cum)).astype(BF16)
    kr = (kf * jnp.exp2(end - cum)).astype(BF16)
    st_add = _dot(v.astype(F32).T.astype(BF16), kr)
    return o_intra, q_in, jnp.exp2(end), st_add


def _hgrn_state_kernel(kf_ref, gf_ref, kb_ref, gb_ref, v_ref, tri_ref, sf_ref, sb_ref):
    n = v_ref.shape[0]
    heads = sf_ref.shape[0]
    head = lambda hh: slice(hh * HGRN_DK, (hh + 1) * HGRN_DK)
    vts = [v_ref[:, head(hh)].astype(F32).T.astype(BF16) for hh in range(heads)]
    for d, (k_ref, g_ref, s_ref) in enumerate(((kf_ref, gf_ref, sf_ref), (kb_ref, gb_ref, sb_ref))):
        g_hi, g_lo = _split_bf16(g_ref[...])
        cum = _dot(tri_ref[d], g_hi) + _dot(tri_ref[d], g_lo)
        end = cum[n - 1:n] if d == 0 else cum[0:1]
        kr = (k_ref[...].astype(F32) * jnp.exp(end - cum)).astype(BF16)
        for hh in range(heads):
            s_ref[hh] = _dot(vts[hh], kr[:, head(hh)])


def _hgrn_state(kf, gf, kb, gb, v):
    b, n, w = v.shape
    heads = w // HGRN_DK
    idx = np.arange(n)
    tri = jnp.asarray(np.stack([idx[None, :] <= idx[:, None], idx[None, :] >= idx[:, None]]), BF16)
    tile = pl.BlockSpec((None, n, w), lambda i: (i, 0, 0))
    st = pl.BlockSpec((None, heads, HGRN_DK, HGRN_DK), lambda i: (i, 0, 0, 0))
    shape = jax.ShapeDtypeStruct((b, heads, HGRN_DK, HGRN_DK), F32)
    return pl.pallas_call(
        _hgrn_state_kernel,
        grid=(b,),
        in_specs=[tile] * 5 + [pl.BlockSpec((2, n, n), lambda i: (0, 0, 0))],
        out_specs=[st, st],
        out_shape=[shape, shape],
        compiler_params=_params("parallel"),
        name="hgrn_state",
    )(kf, gf, kb, gb, v, tri)


def _hgrn_scan_kernel(qf_ref, vf_ref, kf_ref, gf_ref, qb_ref, vb_ref, kb_ref, gb_ref,
                      s0f_ref, s0b_ref, mf_ref, mb_ref, mkf_ref, mkb_ref,
                      of_ref, ob_ref, stf_ref, stb_ref, *, chunk):
    @pl.when(pl.program_id(2) == 0)
    def _():
        stf_ref[...] = s0f_ref[...]
        stb_ref[...] = s0b_ref[...]

    n_chunks = qf_ref.shape[0] // chunk
    mf, mb = mf_ref[...], mb_ref[...]
    bodies = []
    for ci in range(n_chunks):
        bodies.append((slice(ci * chunk, (ci + 1) * chunk), False))
        rc = n_chunks - 1 - ci
        bodies.append((slice(rc * chunk, (rc + 1) * chunk), True))
    refs = {False: (qf_ref, kf_ref, gf_ref, vf_ref, mf, mkf_ref, of_ref),
            True: (qb_ref, kb_ref, gb_ref, vb_ref, mb, mkb_ref, ob_ref)}
    tables = [_hgrn_tables(refs[rev][2][sl, :], refs[rev][4]) for sl, rev in bodies]
    parts = []
    for (sl, rev), e in zip(bodies, tables):
        q_ref, k_ref, _, v_ref, _, masks, _ = refs[rev]
        parts.append(_hgrn_intra(q_ref[sl, :], k_ref[sl, :], v_ref[sl, :], e, masks, rev))
    st = {False: stf_ref[...], True: stb_ref[...]}
    for (sl, rev), (o_intra, q_in, dec_end, st_add) in zip(bodies, parts):
        refs[rev][6][sl, :] = o_intra + _dot_nt(q_in, st[rev].astype(BF16))
        st[rev] = st[rev] * dec_end + st_add
    stf_ref[...] = st[False]
    stb_ref[...] = st[True]


def _hgrn_scan(q, v, kf, gf, kb, gb, s0f, s0b, step, chunk):
    b, n, w = q.shape
    heads = w // HGRN_DK
    nt = n // step
    mf, mkf = _chunk_tables(chunk, False)
    mb, mkb = _chunk_tables(chunk, True)
    fwd = pl.BlockSpec((None, step, HGRN_DK), lambda i, hh, t: (i, t, hh))
    bwd = pl.BlockSpec((None, step, HGRN_DK), lambda i, hh, t: (i, nt - 1 - t, hh))
    st = pl.BlockSpec((None, None, HGRN_DK, HGRN_DK), lambda i, hh, t: (i, hh, 0, 0))
    c2 = pl.BlockSpec(mf.shape, lambda i, hh, t: (0, 0))
    c3 = pl.BlockSpec(mkf.shape, lambda i, hh, t: (0, 0, 0))
    out = jax.ShapeDtypeStruct((b, n, w), F32)
    return pl.pallas_call(
        functools.partial(_hgrn_scan_kernel, chunk=chunk),
        grid=(b, heads, nt),
        in_specs=[fwd] * 4 + [bwd] * 4 + [st, st, c2, c2, c3, c3],
        out_specs=[fwd, bwd],
        out_shape=[out, out],
        scratch_shapes=[pltpu.VMEM((HGRN_DK, HGRN_DK), F32)] * 2,
        compiler_params=_params("parallel", "parallel", "arbitrary"),
        name="hgrn_scan",
    )(q, v, kf, gf, q, v, kb, gb, s0f, s0b, mf, mb, mkf, mkb)


def kernel(x, c, ctx, c_ctx, ada_w, ada_b, norm_mix_w, norm_ffn_w, attn_w_qkv, attn_q_norm,
           attn_k_norm, attn_w_o, hgrn_w_in, hgrn_lb_logits, hgrn_out_norm, hgrn_w_o,
           ffn_w_in, ffn_w_out, final_norm_w):
    b, n, d = x.shape
    m = ctx.shape[1]
    depth = ada_w.shape[0]
    tm = min(512, n)

    rows = -(-(b + 1) // 8) * 8
    cc = jnp.zeros((rows, d), F32).at[:b].set(c).at[b].set(c_ctx)
    mod = _adaln(cc, ada_w, ada_b)

    def mods(i, lat):
        parts = []
        for s in range(6):
            blk = mod[i, :, s * d:(s + 1) * d]
            if lat:
                parts.append(blk[:b, None, :])
            else:
                parts.append(jnp.broadcast_to(blk[b][None, None, :], (b, 1, d)))
        return parts

    h, hc = x, ctx
    for i in range(depth):
        last = i == depth - 1
        j = i // 2
        sh1, sc1, gt1, sh2, sc2, gt2 = mods(i, True)
        csh1, csc1, cgt1, csh2, csc2, cgt2 = mods(i, False)
        nmw = norm_mix_w[i][None, :]
        nfw = norm_ffn_w[i][None, :]
        wi = ffn_w_in[i].astype(BF16)
        wout = ffn_w_out[i].astype(BF16)
        fin = final_norm_w[None, :] if last else None
        if i % 2 == 0:
            w_qkv = attn_w_qkv[j].astype(BF16)
            wo = attn_w_o[j].astype(BF16)
            qn = jnp.tile(attn_q_norm[j], LANES // HEAD_DIM)[None, :]
            kn = jnp.tile(attn_k_norm[j], LANES // HEAD_DIM)[None, :]
            roped = ATTN_HEADS * HEAD_DIM + ATTN_KV_HEADS * HEAD_DIM
            w_rope = jnp.concatenate([w_qkv, w_qkv[:, _swap16_index(roped)]], axis=1)
            q_l, k_l, v_l = _qkv(h, nmw, sh1, sc1, w_rope, qn, kn, _rope_tables(n), tm)
            q_c, k_c, v_c = _qkv(hc, nmw, csh1, csc1, w_qkv, qn, kn, None, m)
            o_l = _attention(q_l, [k_c, k_l], [v_c, v_l], min(512, n),
                             768 if (m + n) % 768 == 0 else m)
            h = _post(h, [o_l], wo, gt1, nfw, sh2, sc2, gt2, wi, wout, fin, "attn", tm)
            if not last:
                o_c = _attention(q_c, [k_c], [v_c], m, m)
                hc = _post(hc, [o_c], wo, cgt1, nfw, csh2, csc2, cgt2, wi, wout, None, "attn", m)
        else:
            w_in = hgrn_w_in[j].astype(BF16)
            wo = hgrn_w_o[j].astype(BF16)
            on = hgrn_out_norm[j][None, :]
            q, gate, gf, kf, gb, kb, v = _hgrn_in(h, nmw, sh1, sc1, w_in, hgrn_lb_logits, i, tm)
            cq, cgate, cgf, ckf, cgb, ckb, cv = _hgrn_in(hc, nmw, csh1, csc1, w_in,
                                                         hgrn_lb_logits, i, m)
            if last:
                s0f, s0b = _hgrn_state(ckf, cgf, ckb, cgb, cv)
            else:
                zero = jnp.zeros((b, HGRN_HEADS, HGRN_DK, HGRN_DK), F32)
                oc_f, oc_b = _hgrn_scan(cq, cv, ckf, cgf, ckb, cgb, zero, zero, m, HGRN_CHUNK)
                s0f, s0b = _hgrn_state(ckf, cgf, ckb, cgb, cv)
                hc = _post(hc, [oc_f, oc_b, cgate, on], wo, cgt1, nfw, csh2, csc2, cgt2, wi, wout,
                           None, "hgrn", m)
            o_f, o_b = _hgrn_scan(q, v, kf, gf, kb, gb, s0f, s0b, min(HGRN_STEP, n), HGRN_CHUNK)
            h = _post(h, [o_f, o_b, gate, on], wo, gt1, nfw, sh2, sc2, gt2, wi, wout, fin, "hgrn", tm)
    return h
```

```python
import functools

import numpy as np
import jax
import jax.numpy as jnp
from jax import lax
from jax.experimental import pallas as pl
from jax.experimental.pallas import tpu as pltpu

F32 = jnp.float32
BF16 = jnp.bfloat16

NORM_EPS = 1e-6
GRID_W = 64
ATTN_HEADS = 16
ATTN_KV_HEADS = 4
HEAD_DIM = 64
ATTN_GROUP = ATTN_HEADS // ATTN_KV_HEADS
ROPE_THETA = 10000.0
HGRN_HEADS = 8
HGRN_DK = 128

LANES = 128
SUBLANES = 8
MXU_ROWS = 256
VT_ROWS = LANES
VMEM_LIMIT = 56 * 1024 * 1024

ROW_TILE = 2 * MXU_ROWS
ATTN_TQ = 2 * MXU_ROWS
ATTN_TK = 3 * MXU_ROWS
HGRN_CHUNK = 128
HGRN_STEP = 2048
LOG2_E = float(np.log2(np.e))


def _params(*sem):
    return pltpu.CompilerParams(dimension_semantics=sem, vmem_limit_bytes=VMEM_LIMIT)


def _dot(a, b):
    return jnp.dot(a, b, preferred_element_type=F32)


def _dot_nt(a, b):
    return lax.dot_general(a, b, (((1,), (1,)), ((), ())), preferred_element_type=F32)


def _sigmoid(x):
    return 1.0 / (1.0 + jnp.exp(-x))


def _rms(x):
    return x * lax.rsqrt(jnp.mean(x * x, axis=-1, keepdims=True) + NORM_EPS)


def _row_subtiles(rows):
    size = MXU_ROWS if rows % MXU_ROWS == 0 else rows
    return [slice(r, r + size) for r in range(0, rows, size)]


def _adaln_kernel(c_ref, w_ref, b_ref, o_ref):
    c = c_ref[...]
    a = (c * _sigmoid(c)).astype(BF16)
    o_ref[...] = _dot(a, w_ref[...].astype(BF16)) + b_ref[...]


def _adaln(cc, ada_w, ada_b):
    depth, d, n = ada_w.shape
    rows = cc.shape[0]
    tn = 1536
    return pl.pallas_call(
        _adaln_kernel,
        grid=(depth, n // tn),
        in_specs=[
            pl.BlockSpec((rows, d), lambda i, j: (0, 0)),
            pl.BlockSpec((None, d, tn), lambda i, j: (i, 0, j)),
            pl.BlockSpec((None, 1, tn), lambda i, j: (i, 0, j)),
        ],
        out_specs=pl.BlockSpec((None, rows, tn), lambda i, j: (i, 0, j)),
        out_shape=jax.ShapeDtypeStruct((depth, rows, n), F32),
        compiler_params=_params("parallel", "parallel"),
        name="adaln",
    )(cc, ada_w, ada_b.reshape(depth, 1, n))


def _swap16_index(n):
    j = np.arange(n)
    return np.where((j % 32) < 16, j + 16, j - 16)


def _head_norm_rope(x, x_sw, w, w_sw, cos, sin):
    lane = lax.broadcasted_iota(jnp.int32, x.shape, 1)
    lo = lane < HEAD_DIM
    sq = x * x
    s_lo = jnp.sum(jnp.where(lo, sq, 0.0), axis=-1, keepdims=True)
    s_hi = jnp.sum(jnp.where(lo, 0.0, sq), axis=-1, keepdims=True)
    ms = jnp.where(lo, s_lo, s_hi) * (1.0 / HEAD_DIM)
    rinv = lax.rsqrt(ms + NORM_EPS)
    xn = x * rinv * w
    if cos is None:
        return xn
    return xn * cos + (x_sw * rinv * w_sw) * sin


def _qkv_kernel(*refs, rope, q_scale):
    if rope:
        (x_ref, nw_ref, sh_ref, sc_ref, w_ref, qn_ref, kn_ref, qns_ref, kns_ref, cos_ref, sin_ref,
         qt_ref, k_ref, vt_ref) = refs
        cos, sin = cos_ref[...], sin_ref[...]
        qns, kns = qns_ref[...], kns_ref[...]
    else:
        x_ref, nw_ref, sh_ref, sc_ref, w_ref, qn_ref, kn_ref, qt_ref, k_ref, vt_ref = refs
        cos = sin = qns = kns = None
    x = x_ref[...]
    hn = (_rms(x) * nw_ref[...]) * (1.0 + sc_ref[...]) + sh_ref[...]
    z = _dot(hn.astype(BF16), w_ref[...])
    tm = z.shape[0]
    qw = qt_ref.shape[0]
    kw = k_ref.shape[0] * HEAD_DIM
    sw = qw + 2 * kw

    def slab(base, s):
        return z[:, base + s * LANES:base + (s + 1) * LANES]

    for s in range(qw // LANES):
        qn = _head_norm_rope(slab(0, s), slab(sw, s) if rope else None,
                             qn_ref[...], qns, cos, sin) * q_scale
        qt_ref[s * LANES:(s + 1) * LANES, :] = qn.T.astype(BF16)
    for s in range(kw // LANES):
        kn = _head_norm_rope(slab(qw, s), slab(sw + qw, s) if rope else None,
                             kn_ref[...], kns, cos, sin)
        k_ref[2 * s] = kn[:, :HEAD_DIM].astype(BF16)
        k_ref[2 * s + 1] = kn[:, HEAD_DIM:].astype(BF16)
    row = lax.broadcasted_iota(jnp.int32, (VT_ROWS - HEAD_DIM, tm), 0)
    pad = jnp.where(row == 0, 1.0, 0.0).astype(BF16)
    for s in range(kw // LANES):
        vt = z[:, qw + kw + s * LANES:qw + kw + (s + 1) * LANES].T.astype(BF16)
        for half in range(2):
            vt_ref[2 * s + half, :HEAD_DIM, :] = vt[half * HEAD_DIM:(half + 1) * HEAD_DIM]
            vt_ref[2 * s + half, HEAD_DIM:, :] = pad


def _qkv(h, nw, sh, sc, w_qkv, qn, kn, rope_tabs, tm):
    b, t, d = h.shape
    n = w_qkv.shape[1]
    qw = ATTN_HEADS * HEAD_DIM
    kw = ATTN_KV_HEADS * HEAD_DIM
    rope = rope_tabs is not None
    row = lambda i, j: (i, 0, 0)
    in_specs = [
        pl.BlockSpec((None, tm, d), lambda i, j: (i, j, 0)),
        pl.BlockSpec((1, d), lambda i, j: (0, 0)),
        pl.BlockSpec((None, 1, d), row),
        pl.BlockSpec((None, 1, d), row),
        pl.BlockSpec((d, n), lambda i, j: (0, 0)),
        pl.BlockSpec((1, LANES), lambda i, j: (0, 0)),
        pl.BlockSpec((1, LANES), lambda i, j: (0, 0)),
    ]
    args = [h, nw, sh, sc, w_qkv, qn, kn]
    if rope:
        swap = _swap16_index(LANES)
        in_specs += [pl.BlockSpec((1, LANES), lambda i, j: (0, 0))] * 2
        in_specs += [pl.BlockSpec((tm, LANES), lambda i, j: (j, 0))] * 2
        args += [qn[:, swap], kn[:, swap]] + list(rope_tabs)
    return pl.pallas_call(
        functools.partial(_qkv_kernel, rope=rope, q_scale=HEAD_DIM ** -0.5 * LOG2_E),
        grid=(b, t // tm),
        in_specs=in_specs,
        out_specs=[
            pl.BlockSpec((None, qw, tm), lambda i, j: (i, 0, j)),
            pl.BlockSpec((None, ATTN_KV_HEADS, tm, HEAD_DIM), lambda i, j: (i, 0, j, 0)),
            pl.BlockSpec((None, ATTN_KV_HEADS, VT_ROWS, tm), lambda i, j: (i, 0, 0, j)),
        ],
        out_shape=[
            jax.ShapeDtypeStruct((b, qw, t), BF16),
            jax.ShapeDtypeStruct((b, ATTN_KV_HEADS, t, HEAD_DIM), BF16),
            jax.ShapeDtypeStruct((b, ATTN_KV_HEADS, VT_ROWS, t), BF16),
        ],
        compiler_params=_params("parallel", "parallel"),
        name="qkv_rope" if rope else "qkv_ctx",
    )(*args)


def _rope_tables(n_tokens):
    axis_dim = HEAD_DIM // 2
    pos = np.arange(n_tokens)
    inv_freq = ROPE_THETA ** (-(np.arange(axis_dim // 2, dtype=np.float32) * 2.0 / axis_dim))
    ang_r = (pos // GRID_W).astype(np.float32)[:, None] * inv_freq
    ang_c = (pos % GRID_W).astype(np.float32)[:, None] * inv_freq
    cos = np.concatenate([np.cos(ang_r)] * 2 + [np.cos(ang_c)] * 2, axis=1)
    sin = np.concatenate([-np.sin(ang_r), np.sin(ang_r), -np.sin(ang_c), np.sin(ang_c)], axis=1)
    reps = LANES // HEAD_DIM
    return (jnp.asarray(np.tile(cos, (1, reps)), F32), jnp.asarray(np.tile(sin, (1, reps)), F32))


def _key_span(refs, lo, hi, axis):
    out, base = [], 0
    for r in refs:
        n = r.shape[axis]
        a, b = max(lo - base, 0), min(hi - base, n)
        if a < b:
            out.append(r[a:b, :] if axis == 0 else r[:, a:b])
        base += n
    return out[0] if len(out) == 1 else jnp.concatenate(out, axis=axis)


def _attn_kernel(qt_ref, *refs, tk, n_parts):
    k_refs, vt_refs = refs[:n_parts], refs[n_parts:2 * n_parts]
    o_ref, bufs = refs[2 * n_parts], refs[2 * n_parts + 1:]
    nk = sum(r.shape[0] for r in k_refs) // tk
    tq = qt_ref.shape[1]
    s_bufs, p_bufs = bufs[0:2], bufs[2:4]
    qt = qt_ref[...]
    qs = jnp.concatenate([qt[g * HEAD_DIM:(g + 1) * HEAD_DIM] for g in range(ATTN_GROUP)], axis=1)
    cols = qs.shape[1]
    m = jnp.full((1, cols), -jnp.inf, F32)
    acc = jnp.zeros((VT_ROWS, cols), F32)
    alpha = [None, None]
    for t in range(nk + 2):
        if t < nk:
            s_bufs[t % 2][...] = _dot(_key_span(k_refs, t * tk, (t + 1) * tk, 0), qs)
        j = t - 1
        if 0 <= j < nk:
            s = s_bufs[j % 2][...]
            m_new = jnp.maximum(m, jnp.max(s, axis=0, keepdims=True))
            p_bufs[j % 2][...] = jnp.exp2(s - m_new).astype(BF16)
            alpha[j % 2] = jnp.exp2(m - m_new)
            m = m_new
        j = t - 2
        if 0 <= j < nk:
            acc = alpha[j % 2] * acc + _dot(
                _key_span(vt_refs, j * tk, (j + 1) * tk, 1), p_bufs[j % 2][...])
    o = (acc / acc[HEAD_DIM:HEAD_DIM + 1]).T
    o_ref[...] = jnp.concatenate(
        [o[g * tq:(g + 1) * tq, :HEAD_DIM] for g in range(ATTN_GROUP)], axis=1).astype(BF16)


def _attention(qt, ks, vts, tq, tk):
    b, qw, t = qt.shape
    gw = ATTN_GROUP * HEAD_DIM
    cols = ATTN_GROUP * tq
    k_specs = [pl.BlockSpec((None, None, k.shape[2], HEAD_DIM), lambda i, h, j: (i, h, 0, 0))
               for k in ks]
    vt_specs = [pl.BlockSpec((None, None, VT_ROWS, vt.shape[3]), lambda i, h, j: (i, h, 0, 0))
                for vt in vts]
    return pl.pallas_call(
        functools.partial(_attn_kernel, tk=tk, n_parts=len(ks)),
        grid=(b, ATTN_KV_HEADS, t // tq),
        in_specs=[pl.BlockSpec((None, gw, tq), lambda i, h, j: (i, h, j))] + k_specs + vt_specs,
        out_specs=pl.BlockSpec((None, tq, gw), lambda i, h, j: (i, j, h)),
        out_shape=jax.ShapeDtypeStruct((b, t, qw), BF16),
        scratch_shapes=[pltpu.VMEM((tk, cols), F32)] * 2 + [pltpu.VMEM((tk, cols), BF16)] * 2,
        compiler_params=_params("parallel", "parallel", "parallel"),
        name="attention",
    )(qt, *ks, *vts)


def _post_kernel(*refs, mode, final, ffn_splits):
    if mode == "attn":
        (h_ref, y_ref, wo_ref, g1_ref, nw_ref, sh_ref, sc_ref, g2_ref, wi_ref, wout_ref) = refs[:10]
        rest = refs[10:]
    else:
        (h_ref, of_ref, ob_ref, gate_ref, on_ref, wo_ref, g1_ref, nw_ref, sh_ref, sc_ref, g2_ref,
         wi_ref, wout_ref) = refs[:13]
        rest = refs[13:]
    if final:
        fw_ref, out_ref = rest
    else:
        (out_ref,) = rest
    hidden = wout_ref.shape[0]
    if mode == "attn":
        y_in = y_ref[...]
    else:
        o = of_ref[...] + ob_ref[...]
        gate = gate_ref[...]
        parts = []
        for hh in range(o.shape[1] // HGRN_DK):
            sl = slice(hh * HGRN_DK, (hh + 1) * HGRN_DK)
            parts.append(_rms(o[:, sl]) * on_ref[:, sl] * _sigmoid(gate[:, sl]))
        y_in = jnp.concatenate(parts, axis=1).astype(BF16)
    h1 = h_ref[...] + g1_ref[...] * _dot(y_in, wo_ref[...])
    hn = ((_rms(h1) * nw_ref[...]) * (1.0 + sc_ref[...]) + sh_ref[...]).astype(BF16)
    acc = None
    off = 0
    for width in ffn_splits:
        a = _dot(hn, wi_ref[:, off:off + width])
        u = _dot(hn, wi_ref[:, hidden + off:hidden + off + width])
        gu = (a * _sigmoid(a) * u).astype(BF16)
        part = _dot(gu, wout_ref[off:off + width, :])
        acc = part if acc is None else acc + part
        off += width
    h2 = h1 + g2_ref[...] * acc
    if final:
        h2 = _rms(h2) * fw_ref[...]
    out_ref[...] = h2


def _post(h, mixer_in, wo, g1, nw, sh, sc, g2, wi, wout, final_w, mode, tm):
    b, t, d = h.shape
    hidden = wout.shape[0]
    tile = pl.BlockSpec((None, tm, d), lambda i, j: (i, j, 0))
    row = pl.BlockSpec((None, 1, d), lambda i, j: (i, 0, 0))
    const = lambda shape: pl.BlockSpec(shape, lambda i, j: (0,) * len(shape),
                                       pipeline_mode=pl.Buffered(1))
    if mode == "attn":
        mix_specs = [pl.BlockSpec((None, tm, mixer_in[0].shape[-1]), lambda i, j: (i, j, 0))]
    else:
        mix_specs = [tile, tile, tile, const((1, d))]
    in_specs = [tile] + mix_specs + [const(wo.shape), row, const((1, d)), row, row, row,
                                     const(wi.shape), const(wout.shape)]
    args = [h] + list(mixer_in) + [wo, g1, nw, sh, sc, g2, wi, wout]
    final = final_w is not None
    if final:
        in_specs.append(const((1, d)))
        args.append(final_w)
    splits = (1536, hidden - 1536) if hidden > 1536 else (hidden,)
    return pl.pallas_call(
        functools.partial(_post_kernel, mode=mode, final=final, ffn_splits=splits),
        grid=(b, t // tm),
        in_specs=in_specs,
        out_specs=tile,
        out_shape=jax.ShapeDtypeStruct((b, t, d), F32),
        compiler_params=_params("parallel", "parallel"),
        name="post_" + mode + ("_final" if final else ""),
    )(*args)


def _hgrn_in_kernel(x_ref, nw_ref, sh_ref, sc_ref, w_ref, lbl_ref,
                    q_ref, gate_ref, gf_ref, kf_ref, gb_ref, kb_ref, v_ref, *, layer):
    wd = q_ref.shape[-1]
    lg = lbl_ref[...]
    e = jnp.exp(lg - jnp.max(lg, axis=0, keepdims=True))
    p = e / jnp.sum(e, axis=0, keepdims=True)
    lb = jnp.sum(p[1:layer + 1], axis=0, keepdims=True) if layer > 0 else jnp.zeros_like(p[0:1])

    for rows in _row_subtiles(x_ref.shape[0]):
        x = x_ref[rows, :]
        hn = ((_rms(x) * nw_ref[...]) * (1.0 + sc_ref[...]) + sh_ref[...]).astype(BF16)

        def proj(i):
            return _dot(hn, w_ref[:, i * wd:(i + 1) * wd])

        q_ref[rows, :] = proj(0).astype(BF16)
        gate_ref[rows, :] = proj(1)
        for i, (g_ref, k_ref) in ((2, (gf_ref, kf_ref)), (3, (gb_ref, kb_ref))):
            f = lb + (1.0 - lb) * _sigmoid(proj(i))
            g_ref[rows, :] = jnp.log(f)
            k_ref[rows, :] = (1.0 - f).astype(BF16)
        v_ref[rows, :] = proj(4).astype(BF16)


def _hgrn_in(h, nw, sh, sc, w_in, lb_logits, layer, tm):
    b, t, d = h.shape
    wd = w_in.shape[1] // 5
    tile = lambda: pl.BlockSpec((None, tm, wd), lambda i, j: (i, j, 0))
    row = pl.BlockSpec((None, 1, d), lambda i, j: (i, 0, 0))
    depth = lb_logits.shape[0]
    dts = (BF16, F32, F32, BF16, F32, BF16, BF16)
    return pl.pallas_call(
        functools.partial(_hgrn_in_kernel, layer=layer),
        grid=(b, t // tm),
        in_specs=[
            pl.BlockSpec((None, tm, d), lambda i, j: (i, j, 0)),
            pl.BlockSpec((1, d), lambda i, j: (0, 0)),
            row, row,
            pl.BlockSpec(w_in.shape, lambda i, j: (0, 0), pipeline_mode=pl.Buffered(1)),
            pl.BlockSpec((depth, wd), lambda i, j: (0, 0)),
        ],
        out_specs=[tile() for _ in dts],
        out_shape=[jax.ShapeDtypeStruct((b, t, wd), dt) for dt in dts],
        compiler_params=_params("parallel", "parallel"),
        name="hgrn_in",
    )(h, nw, sh, sc, w_in, lb_logits)


def _chunk_tables(c, rev):
    levels = int(np.log2(c))
    idx = np.arange(c)
    t, j = idx[:, None], idx[None, :]
    mats = [(j <= t)]
    masks = [np.eye(c, dtype=bool)]
    for lv in range(1, levels + 1):
        blk = 2 ** lv
        mid = (idx // blk) * blk + blk // 2
        upper = (idx % blk) >= blk // 2
        m_up = upper[:, None] & (j >= mid[:, None]) & (j <= t)
        m_lo = (~upper)[:, None] & (j > t) & (j <= mid[:, None] - 1)
        if blk <= SUBLANES:
            mats.append(m_up | m_lo)
        same = (idx[:, None] // blk) == (idx[None, :] // blk)
        masks.append(same & upper[:, None] & (~upper)[None, :])
    mats = np.stack(mats).astype(np.float32)
    masks = np.stack(masks).astype(np.float32)
    if rev:
        mats = mats[:, ::-1, ::-1]
        masks = masks[:, ::-1, ::-1]
    return (jnp.asarray(mats.reshape(-1, c), BF16), jnp.asarray(masks, F32))


def _split_bf16(g):
    hi = g.astype(BF16)
    return hi, (g - hi.astype(F32)).astype(BF16)


def _level_exponents(cum, blk, rev):
    half = blk // 2
    parts = []
    for b0 in range(0, cum.shape[0], blk):
        lo, up = cum[b0:b0 + half], cum[b0 + half:b0 + blk]
        if rev:
            ref = cum[b0 + half:b0 + half + 1]
            parts += [lo - ref, ref - up]
        else:
            ref = cum[b0 + half - 1:b0 + half]
            parts += [ref - lo, up - ref]
    return jnp.concatenate(parts, axis=0)


def _hgrn_tables(g, mst):
    g_hi, g_lo = _split_bf16(g * LOG2_E)
    return _dot(mst, g_hi) + _dot(mst, g_lo)


def _hgrn_intra(q, k, v, e, masks, rev):
    c = q.shape[0]
    cum = e[0:c]
    end = cum[0:1] if rev else cum[c - 1:c]
    qf, kf = q.astype(F32), k.astype(F32)
    a = masks[0] * _dot_nt(q, k)
    for lv in range(1, masks.shape[0]):
        blk = 2 ** lv
        ex = e[lv * c:(lv + 1) * c] if blk <= SUBLANES else _level_exponents(cum, blk, rev)
        d = jnp.exp2(ex)
        a = a + masks[lv] * _dot((qf * d).astype(BF16), (kf * d).T.astype(BF16))
    o_intra = _dot(a.astype(BF16), v)
    q_in = (qf * jnp.exp2(cum)).astype(BF16)
    kr = (kf * jnp.exp2(end - cum)).astype(BF16)
    st_add = _dot(v.astype(F32).T.astype(BF16), kr)
    return o_intra, q_in, jnp.exp2(end), st_add


def _hgrn_state_kernel(kf_ref, gf_ref, kb_ref, gb_ref, v_ref, tri_ref, sf_ref, sb_ref):
    n = v_ref.shape[0]
    heads = sf_ref.shape[0]
    head = lambda hh: slice(hh * HGRN_DK, (hh + 1) * HGRN_DK)
    vts = [v_ref[:, head(hh)].astype(F32).T.astype(BF16) for hh in range(heads)]
    for d, (k_ref, g_ref, s_ref) in enumerate(((kf_ref, gf_ref, sf_ref), (kb_ref, gb_ref, sb_ref))):
        g_hi, g_lo = _split_bf16(g_ref[...])
        cum = _dot(tri_ref[d], g_hi) + _dot(tri_ref[d], g_lo)
        end = cum[n - 1:n] if d == 0 else cum[0:1]
        kr = (k_ref[...].astype(F32) * jnp.exp(end - cum)).astype(BF16)
        for hh in range(heads):
            s_ref[hh] = _dot(vts[hh], kr[:, head(hh)])


def _hgrn_state(kf, gf, kb, gb, v):
    b, n, w = v.shape
    heads = w // HGRN_DK
    idx = np.arange(n)
    tri = jnp.asarray(np.stack([idx[None, :] <= idx[:, None], idx[None, :] >= idx[:, None]]), BF16)
    tile = pl.BlockSpec((None, n, w), lambda i: (i, 0, 0))
    st = pl.BlockSpec((None, heads, HGRN_DK, HGRN_DK), lambda i: (i, 0, 0, 0))
    shape = jax.ShapeDtypeStruct((b, heads, HGRN_DK, HGRN_DK), F32)
    return pl.pallas_call(
        _hgrn_state_kernel,
        grid=(b,),
        in_specs=[tile] * 5 + [pl.BlockSpec((2, n, n), lambda i: (0, 0, 0))],
        out_specs=[st, st],
        out_shape=[shape, shape],
        compiler_params=_params("parallel"),
        name="hgrn_state",
    )(kf, gf, kb, gb, v, tri)


def _hgrn_scan_kernel(qf_ref, vf_ref, kf_ref, gf_ref, qb_ref, vb_ref, kb_ref, gb_ref,
                      s0f_ref, s0b_ref, mf_ref, mb_ref, mkf_ref, mkb_ref,
                      of_ref, ob_ref, stf_ref, stb_ref, *, chunk):
    @pl.when(pl.program_id(2) == 0)
    def _():
        stf_ref[...] = s0f_ref[...]
        stb_ref[...] = s0b_ref[...]

    n_chunks = qf_ref.shape[0] // chunk
    mf, mb = mf_ref[...], mb_ref[...]
    bodies = []
    for ci in range(n_chunks):
        bodies.append((slice(ci * chunk, (ci + 1) * chunk), False))
        rc = n_chunks - 1 - ci
        bodies.append((slice(rc * chunk, (rc + 1) * chunk), True))
    refs = {False: (qf_ref, kf_ref, gf_ref, vf_ref, mf, mkf_ref, of_ref),
            True: (qb_ref, kb_ref, gb_ref, vb_ref, mb, mkb_ref, ob_ref)}
    tables = [_hgrn_tables(refs[rev][2][sl, :], refs[rev][4]) for sl, rev in bodies]
    parts = []
    for (sl, rev), e in zip(bodies, tables):
        q_ref, k_ref, _, v_ref, _, masks, _ = refs[rev]
        parts.append(_hgrn_intra(q_ref[sl, :], k_ref[sl, :], v_ref[sl, :], e, masks, rev))
    st = {False: stf_ref[...], True: stb_ref[...]}
    for (sl, rev), (o_intra, q_in, dec_end, st_add) in zip(bodies, parts):
        refs[rev][6][sl, :] = o_intra + _dot_nt(q_in, st[rev].astype(BF16))
        st[rev] = st[rev] * dec_end + st_add
    stf_ref[...] = st[False]
    stb_ref[...] = st[True]


def _hgrn_scan(q, v, kf, gf, kb, gb, s0f, s0b, step, chunk):
    b, n, w = q.shape
    heads = w // HGRN_DK
    nt = n // step
    mf, mkf = _chunk_tables(chunk, False)
    mb, mkb = _chunk_tables(chunk, True)
    fwd = pl.BlockSpec((None, step, HGRN_DK), lambda i, hh, t: (i, t, hh))
    bwd = pl.BlockSpec((None, step, HGRN_DK), lambda i, hh, t: (i, nt - 1 - t, hh))
    st = pl.BlockSpec((None, None, HGRN_DK, HGRN_DK), lambda i, hh, t: (i, hh, 0, 0))
    c2 = pl.BlockSpec(mf.shape, lambda i, hh, t: (0, 0))
    c3 = pl.BlockSpec(mkf.shape, lambda i, hh, t: (0, 0, 0))
    out = jax.ShapeDtypeStruct((b, n, w), F32)
    return pl.pallas_call(
        functools.partial(_hgrn_scan_kernel, chunk=chunk),
        grid=(b, heads, nt),
        in_specs=[fwd] * 4 + [bwd] * 4 + [st, st, c2, c2, c3, c3],
        out_specs=[fwd, bwd],
        out_shape=[out, out],
        scratch_shapes=[pltpu.VMEM((HGRN_DK, HGRN_DK), F32)] * 2,
        compiler_params=_params("parallel", "parallel", "arbitrary"),
        name="hgrn_scan",
    )(q, v, kf, gf, q, v, kb, gb, s0f, s0b, mf, mb, mkf, mkb)


def kernel(x, c, ctx, c_ctx, ada_w, ada_b, norm_mix_w, norm_ffn_w, attn_w_qkv, attn_q_norm,
           attn_k_norm, attn_w_o, hgrn_w_in, hgrn_lb_logits, hgrn_out_norm, hgrn_w_o,
           ffn_w_in, ffn_w_out, final_norm_w):
    b, n, d = x.shape
    m = ctx.shape[1]
    depth = ada_w.shape[0]
    tm = min(ROW_TILE, n)

    rows = -(-(b + 1) // 8) * 8
    cc = jnp.zeros((rows, d), F32).at[:b].set(c).at[b].set(c_ctx)
    mod = _adaln(cc, ada_w, ada_b)

    def mods(i, lat):
        parts = []
        for s in range(6):
            blk = mod[i, :, s * d:(s + 1) * d]
            if lat:
                parts.append(blk[:b, None, :])
            else:
                parts.append(jnp.broadcast_to(blk[b][None, None, :], (b, 1, d)))
        return parts

    h, hc = x, ctx
    for i in range(depth):
        last = i == depth - 1
        j = i // 2
        sh1, sc1, gt1, sh2, sc2, gt2 = mods(i, True)
        csh1, csc1, cgt1, csh2, csc2, cgt2 = mods(i, False)
        nmw = norm_mix_w[i][None, :]
        nfw = norm_ffn_w[i][None, :]
        wi = ffn_w_in[i].astype(BF16)
        wout = ffn_w_out[i].astype(BF16)
        fin = final_norm_w[None, :] if last else None
        if i % 2 == 0:
            w_qkv = attn_w_qkv[j].astype(BF16)
            wo = attn_w_o[j].astype(BF16)
            qn = jnp.tile(attn_q_norm[j], LANES // HEAD_DIM)[None, :]
            kn = jnp.tile(attn_k_norm[j], LANES // HEAD_DIM)[None, :]
            roped = ATTN_HEADS * HEAD_DIM + ATTN_KV_HEADS * HEAD_DIM
            w_rope = jnp.concatenate([w_qkv, w_qkv[:, _swap16_index(roped)]], axis=1)
            q_l, k_l, v_l = _qkv(h, nmw, sh1, sc1, w_rope, qn, kn, _rope_tables(n), tm)
            q_c, k_c, v_c = _qkv(hc, nmw, csh1, csc1, w_qkv, qn, kn, None, m)
            o_l = _attention(q_l, [k_c, k_l], [v_c, v_l], min(ATTN_TQ, n),
                             ATTN_TK if (m + n) % ATTN_TK == 0 else m)
            h = _post(h, [o_l], wo, gt1, nfw, sh2, sc2, gt2, wi, wout, fin, "attn", tm)
            if not last:
                o_c = _attention(q_c, [k_c], [v_c], m, m)
                hc = _post(hc, [o_c], wo, cgt1, nfw, csh2, csc2, cgt2, wi, wout, None, "attn", m)
        else:
            w_in = hgrn_w_in[j].astype(BF16)
            wo = hgrn_w_o[j].astype(BF16)
            on = hgrn_out_norm[j][None, :]
            q, gate, gf, kf, gb, kb, v = _hgrn_in(h, nmw, sh1, sc1, w_in, hgrn_lb_logits, i, tm)
            cq, cgate, cgf, ckf, cgb, ckb, cv = _hgrn_in(hc, nmw, csh1, csc1, w_in,
                                                         hgrn_lb_logits, i, m)
            if last:
                s0f, s0b = _hgrn_state(ckf, cgf, ckb, cgb, cv)
            else:
                zero = jnp.zeros((b, HGRN_HEADS, HGRN_DK, HGRN_DK), F32)
                oc_f, oc_b = _hgrn_scan(cq, cv, ckf, cgf, ckb, cgb, zero, zero, m, HGRN_CHUNK)
                s0f, s0b = _hgrn_state(ckf, cgf, ckb, cgb, cv)
                hc = _post(hc, [oc_f, oc_b, cgate, on], wo, cgt1, nfw, csh2, csc2, cgt2, wi, wout,
                           None, "hgrn", m)
            o_f, o_b = _hgrn_scan(q, v, kf, gf, kb, gb, s0f, s0b, min(HGRN_STEP, n), HGRN_CHUNK)
            h = _post(h, [o_f, o_b, gate, on], wo, gt1, nfw, sh2, sc2, gt2, wi, wout, fin, "hgrn", tm)
    return h
```
